```python
import math
import jax, jax.numpy as jnp
from jax import lax
import numpy as np

D_MODEL = 1024
BATCH = 8
SEQ = 4096
DEPTH = 4

RET_HEADS = 4
RET_DK = 128
RET_DV = 128
RET_CHUNK = 128
ROPE_BASE = 10000.0
NSA_HEADS = 8
NSA_KV_GROUPS = 2
NSA_HPG = NSA_HEADS // NSA_KV_GROUPS
NSA_HD = 64
CMP_BLOCK = 32
CMP_STRIDE = 16
SLC_BLOCK = 64
N_SELECT = 8
FORCED_LOCAL = 2
FORCE_BONUS = 1.0e4
WINDOW = 512
Q_BLOCK = 128
HGRN_HEADS = 8
HGRN_DK = D_MODEL // HGRN_HEADS
HGRN_DV = D_MODEL // HGRN_HEADS
HGRN_CHUNK = 64
D_FF = -(-8 * D_MODEL // (3 * 256)) * 256
EPS = 1e-6

EVEN_SIZES = ([RET_HEADS * RET_DK] * 2 + [RET_HEADS * RET_DV] * 2 + [NSA_HEADS * NSA_HD]
              + [NSA_KV_GROUPS * NSA_HD] * 6 + [3 * NSA_HEADS])
EVEN_IN = sum(EVEN_SIZES)
EVEN_MIX = RET_HEADS * RET_DV + NSA_HEADS * NSA_HD
ODD_SIZES = [HGRN_HEADS * HGRN_DK] * 2 + [HGRN_HEADS * HGRN_DV] * 2
ODD_IN = sum(ODD_SIZES)
ODD_MIX = HGRN_HEADS * HGRN_DV
N_EVEN = (DEPTH + 1) // 2
N_ODD = DEPTH // 2

kernel_name = "retnet_nsa_hgrn2_hybrid_trunk"


def _split(a, sizes):
    return jnp.split(a, list(np.cumsum(sizes)[:-1]), axis=-1)


def rmsnorm(x, g):
    xf = x.astype(jnp.float32)
    y = xf * lax.rsqrt(jnp.mean(xf * xf, axis=-1, keepdims=True) + EPS)
    return (y * g.astype(jnp.float32)).astype(x.dtype)


def rotary(x):
    S, d = x.shape[-2], x.shape[-1]
    half = d // 2
    inv = ROPE_BASE ** (-jnp.arange(half, dtype=jnp.float32) / half)
    ang = jnp.arange(S, dtype=jnp.float32)[:, None] * inv[None, :]
    cos, sin = jnp.cos(ang), jnp.sin(ang)
    x1, x2 = x[..., :half], x[..., half:]
    return jnp.concatenate([x1 * cos - x2 * sin, x1 * sin + x2 * cos], axis=-1)


def masked_softmax(s, mask):
    s = jnp.where(mask, s, -jnp.inf)
    m = jnp.max(s, axis=-1, keepdims=True)
    m = jnp.where(jnp.isfinite(m), m, 0.0)
    p = jnp.exp(s - m)
    return p / jnp.maximum(jnp.sum(p, axis=-1, keepdims=True), 1e-30)


def retention_chunkwise(q, k, v):
    B, H, S, dk = q.shape
    dv = v.shape[-1]
    C = min(RET_CHUNK, S)
    NC = S // C
    log_gamma = jnp.log(1.0 - 2.0 ** (-5.0 - jnp.arange(H, dtype=jnp.float32)))
    q = q.reshape(B, H, NC, C, dk)
    k = k.reshape(B, H, NC, C, dk)
    v = v.reshape(B, H, NC, C, dv)
    idx = jnp.arange(C, dtype=jnp.float32)
    diff = idx[:, None] - idx[None, :]
    causal = diff >= 0
    dmat = jnp.where(causal, jnp.exp(log_gamma[:, None, None] * jnp.where(causal, diff, 0.0)), 0.0)
    scores = jnp.einsum('bhncd,bhnsd->bhncs', q, k) * dmat[None, :, None]
    o_inner = jnp.einsum('bhncs,bhnse->bhnce', scores, v)
    k_dec = k * jnp.exp(log_gamma[:, None] * (C - 1 - idx)[None, :])[None, :, None, :, None]
    chunk_kv = jnp.einsum('bhncd,bhnce->nbhde', k_dec, v)
    gamma_c = jnp.exp(log_gamma * C)[None, :, None, None]

    def step(state, kv):
        return gamma_c * state + kv, state

    _, r_prev = lax.scan(step, jnp.zeros((B, H, dk, dv), jnp.float32), chunk_kv)
    q_dec = q * jnp.exp(log_gamma[:, None] * (idx + 1.0)[None, :])[None, :, None, :, None]
    o_cross = jnp.einsum('bhncd,nbhde->bhnce', q_dec, r_prev)
    return (o_inner + o_cross).reshape(B, H, S, dv)


def nsa_compress(kv, pos_emb, w1, w2):
    B, G, S, d = kv.shape
    r = CMP_BLOCK // CMP_STRIDE
    parts = kv.reshape(B, G, S // CMP_STRIDE, CMP_STRIDE, d)
    n = S // CMP_STRIDE - r + 1
    blocks = jnp.concatenate([parts[:, :, j:j + n] for j in range(r)], axis=3)
    flat = (blocks + pos_emb.astype(jnp.float32)).reshape(B, G, n, CMP_BLOCK * d)
    return jax.nn.gelu(flat @ w1.astype(jnp.float32)) @ w2.astype(jnp.float32)


def nsa_attention(q, k_cmp, v_cmp, k_slc, v_slc, k_win, v_win, gates):
    B, G, HPG, S, d = q.shape
    scale = d ** -0.5
    n_cmp = k_cmp.shape[2]
    n_slc = S // SLC_BLOCK
    n_sel = min(N_SELECT, n_slc)
    cs = np.arange(n_cmp) * CMP_STRIDE
    ce = cs + CMP_BLOCK - 1
    ss = np.arange(n_slc) * SLC_BLOCK
    se = ss + SLC_BLOCK - 1
    overlap = jnp.asarray(((cs[:, None] <= se[None, :]) & (ce[:, None] >= ss[None, :])).astype(np.float32))
    cmp_end = jnp.asarray(ce.astype(np.int32))
    ks_blocks = k_slc.reshape(B, G, n_slc, SLC_BLOCK, d)
    vs_blocks = v_slc.reshape(B, G, n_slc, SLC_BLOCK, d)
    kw_pad = jnp.pad(k_win, ((0, 0), (0, 0), (WINDOW, 0), (0, 0)))
    vw_pad = jnp.pad(v_win, ((0, 0), (0, 0), (WINDOW, 0), (0, 0)))
    bi = jnp.arange(B)[:, None, None, None]
    gi = jnp.arange(G)[None, :, None, None]
    jblk = jnp.arange(n_slc)
    offs = jnp.arange(SLC_BLOCK)

    def block(i):
        q0 = i * Q_BLOCK
        t = q0 + jnp.arange(Q_BLOCK)
        qb = lax.dynamic_slice_in_dim(q, q0, Q_BLOCK, axis=3)
        s_c = jnp.einsum('bghqd,bgnd->bghqn', qb, k_cmp) * scale
        p_c = masked_softmax(s_c, cmp_end[None, :] <= t[:, None])
        o_c = jnp.einsum('bghqn,bgnd->bghqd', p_c, v_cmp)
        imp = jnp.einsum('bgqn,nj->bgqj', jnp.sum(p_c, axis=2), overlap)
        bt = (t // SLC_BLOCK)[:, None]
        valid_s = jblk[None, :] <= bt
        forced = (jblk[None, :] == 0) | ((bt - jblk[None, :] >= 0) & (bt - jblk[None, :] < FORCED_LOCAL))
        score = jnp.where(valid_s, imp + jnp.where(forced, FORCE_BONUS, 0.0), -1e30)
        _, idx = lax.top_k(score, n_sel)
        kg = ks_blocks[bi, gi, idx].reshape(B, G, Q_BLOCK, n_sel * SLC_BLOCK, d)
        vg = vs_blocks[bi, gi, idx].reshape(B, G, Q_BLOCK, n_sel * SLC_BLOCK, d)
        kpos = (idx[..., None] * SLC_BLOCK + offs).reshape(B, G, Q_BLOCK, n_sel * SLC_BLOCK)
        s_s = jnp.einsum('bghqd,bgqkd->bghqk', qb, kg) * scale
        p_s = masked_softmax(s_s, (kpos <= t[None, None, :, None])[:, :, None])
        o_s = jnp.einsum('bghqk,bgqkd->bghqd', p_s, vg)
        kw = lax.dynamic_slice_in_dim(kw_pad, q0, WINDOW + Q_BLOCK, axis=2)
        vw = lax.dynamic_slice_in_dim(vw_pad, q0, WINDOW + Q_BLOCK, axis=2)
        spos = q0 - WINDOW + jnp.arange(WINDOW + Q_BLOCK)
        dist = t[:, None] - spos[None, :]
        valid_w = (dist >= 0) & (dist < WINDOW) & (spos[None, :] >= 0)
        s_w = jnp.einsum('bghqd,bgkd->bghqk', qb, kw) * scale
        o_w = jnp.einsum('bghqk,bgkd->bghqd', masked_softmax(s_w, valid_w), vw)
        gb = lax.dynamic_slice_in_dim(gates, q0, Q_BLOCK, axis=4)[..., None]
        return gb[0] * o_c + gb[1] * o_s + gb[2] * o_w

    out = lax.map(block, jnp.arange(S // Q_BLOCK))
    return jnp.transpose(out, (1, 2, 3, 0, 4, 5)).reshape(B, G * HPG, S, d)


def retention_nsa_mixer(h, w_in, w_out, pos_k, w1_k, w2_k, pos_v, w1_v, w2_v):
    B, S, _ = h.shape
    proj = (h @ w_in).astype(jnp.float32)
    (rq, rk, rv, rg, nq, kc, vc, ks, vs, kw, vw, ng) = _split(proj, EVEN_SIZES)
    heads = lambda a, n: jnp.transpose(a.reshape(B, S, n, -1), (0, 2, 1, 3))
    q_r = rotary(heads(rq, RET_HEADS))
    k_r = rotary(heads(rk, RET_HEADS)) * RET_DK ** -0.5
    o_r = retention_chunkwise(q_r, k_r, heads(rv, RET_HEADS))
    mu = jnp.mean(o_r, axis=-1, keepdims=True)
    var = jnp.mean(jnp.square(o_r - mu), axis=-1, keepdims=True)
    o_r = (o_r - mu) * lax.rsqrt(var + 1e-5)
    o_r = jnp.transpose(o_r, (0, 2, 1, 3)).reshape(B, S, RET_HEADS * RET_DV) * jax.nn.silu(rg)
    q_n = jnp.transpose(nq.reshape(B, S, NSA_KV_GROUPS, NSA_HPG, NSA_HD), (0, 2, 3, 1, 4))
    kvh = lambda a: jnp.transpose(a.reshape(B, S, NSA_KV_GROUPS, NSA_HD), (0, 2, 1, 3))
    k_cmp = nsa_compress(kvh(kc), pos_k, w1_k, w2_k)
    v_cmp = nsa_compress(kvh(vc), pos_v, w1_v, w2_v)
    gates = jnp.transpose(jax.nn.sigmoid(ng).reshape(B, S, 3, NSA_KV_GROUPS, NSA_HPG), (2, 0, 3, 4, 1))
    o_n = nsa_attention(q_n, k_cmp, v_cmp, kvh(ks), kvh(vs), kvh(kw), kvh(vw), gates)
    o_n = jnp.transpose(o_n, (0, 2, 1, 3)).reshape(B, S, NSA_HEADS * NSA_HD)
    mixed = jnp.concatenate([o_r, o_n], axis=-1).astype(h.dtype)
    return mixed @ w_out


def hgrn2_chunkwise(q, k, v, log_f):
    B, H, S, dk = q.shape
    dv = v.shape[-1]
    C = min(HGRN_CHUNK, S)
    NC = S // C
    to_chunks = lambda a: jnp.moveaxis(a.reshape(B, H, NC, C, a.shape[-1]), 2, 0)
    causal = jnp.tril(jnp.ones((C, C), dtype=bool))[:, :, None]

    def step(state, xs):
        qc, kc, vc, gc = xs
        b = jnp.cumsum(gc, axis=2)
        diff = b[:, :, :, None, :] - b[:, :, None, :, :]
        decay = jnp.exp(jnp.where(causal, diff, -jnp.inf))
        attn = jnp.sum(qc[:, :, :, None, :] * decay * kc[:, :, None, :, :], axis=-1)
        o = (jnp.einsum('bhnm,bhme->bhne', attn, vc)
             + jnp.einsum('bhnd,bhde->bhne', qc * jnp.exp(b), state))
        b_last = b[:, :, -1:, :]
        new_state = (jnp.exp(b_last[:, :, 0, :])[..., None] * state
                     + jnp.einsum('bhmd,bhme->bhde', kc * jnp.exp(b_last - b), vc))
        return new_state, o

    _, o = lax.scan(step, jnp.zeros((B, H, dk, dv), jnp.float32),
                    (to_chunks(q), to_chunks(k), to_chunks(v), to_chunks(log_f)))
    return jnp.moveaxis(o, 0, 2).reshape(B, H, S, dv)


def hgrn2_mixer(h, w_in, w_out, norm_g, lb):
    B, S, _ = h.shape
    proj = (h @ w_in).astype(jnp.float32)
    qp, fp, ip, gp = _split(proj, ODD_SIZES)
    heads = lambda a: jnp.transpose(a.reshape(B, S, HGRN_HEADS, -1), (0, 2, 1, 3))
    lb = jnp.maximum(lb, 0.0)
    log_f = jnp.logaddexp(jnp.log(lb), jnp.log1p(-lb) + jax.nn.log_sigmoid(fp))
    k = (1.0 - lb) * jax.nn.sigmoid(-fp)
    q = jax.nn.silu(qp) * HGRN_DK ** -0.5
    o = hgrn2_chunkwise(heads(q), heads(k), heads(ip), heads(log_f))
    o = o * lax.rsqrt(jnp.mean(o * o, axis=-1, keepdims=True) + EPS) * norm_g.astype(jnp.float32)
    o = jnp.transpose(o, (0, 2, 1, 3)).reshape(B, S, ODD_MIX) * jax.nn.sigmoid(gp)
    return o.astype(h.dtype) @ w_out


def swiglu(h, w1, w3, w2):
    return (jax.nn.silu(h @ w1) * (h @ w3)) @ w2


def setup_inputs(seed: int = 0) -> dict:
    key = jax.random.key(seed)
    ks = jax.random.split(key, 20)
    nrm = lambda k, shape, s: jax.random.normal(k, shape, jnp.float32) * s
    return {
        "x": nrm(ks[0], (BATCH, SEQ, D_MODEL), 1.0),
        "norm_mix_g": 1.0 + nrm(ks[1], (DEPTH, D_MODEL), 0.05),
        "norm_ffn_g": 1.0 + nrm(ks[2], (DEPTH, D_MODEL), 0.05),
        "final_norm_g": 1.0 + nrm(ks[3], (D_MODEL,), 0.05),
        "even_w_in": nrm(ks[4], (N_EVEN, D_MODEL, EVEN_IN), D_MODEL ** -0.5),
        "even_w_out": nrm(ks[5], (N_EVEN, EVEN_MIX, D_MODEL), EVEN_MIX ** -0.5),
        "cmp_pos_k": nrm(ks[6], (N_EVEN, CMP_BLOCK, NSA_HD), 0.1),
        "cmp_w1_k": nrm(ks[7], (N_EVEN, CMP_BLOCK * NSA_HD, NSA_HD), (CMP_BLOCK * NSA_HD) ** -0.5),
        "cmp_w2_k": nrm(ks[8], (N_EVEN, NSA_HD, NSA_HD), NSA_HD ** -0.5),
        "cmp_pos_v": nrm(ks[9], (N_EVEN, CMP_BLOCK, NSA_HD), 0.1),
        "cmp_w1_v": nrm(ks[10], (N_EVEN, CMP_BLOCK * NSA_HD, NSA_HD), (CMP_BLOCK * NSA_HD) ** -0.5),
        "cmp_w2_v": nrm(ks[11], (N_EVEN, NSA_HD, NSA_HD), NSA_HD ** -0.5),
        "odd_w_in": nrm(ks[12], (N_ODD, D_MODEL, ODD_IN), D_MODEL ** -0.5),
        "odd_w_out": nrm(ks[13], (N_ODD, ODD_MIX, D_MODEL), ODD_MIX ** -0.5),
        "hgrn_norm_g": 1.0 + nrm(ks[14], (N_ODD, HGRN_DV), 0.05),
        "hgrn_lb_logits": nrm(ks[15], (N_ODD, HGRN_HEADS * HGRN_DK), 1.0),
        "ffn_w1": nrm(ks[16], (DEPTH, D_MODEL, D_FF), D_MODEL ** -0.5),
        "ffn_w3": nrm(ks[17], (DEPTH, D_MODEL, D_FF), D_MODEL ** -0.5),
        "ffn_w2": nrm(ks[18], (DEPTH, D_FF, D_MODEL), D_FF ** -0.5),
    }


def reference(x, norm_mix_g, norm_ffn_g, final_norm_g, even_w_in, even_w_out,
              cmp_pos_k, cmp_w1_k, cmp_w2_k, cmp_pos_v, cmp_w1_v, cmp_w2_v,
              odd_w_in, odd_w_out, hgrn_norm_g, hgrn_lb_logits,
              ffn_w1, ffn_w3, ffn_w2):
    lb_sm = jax.nn.softmax(hgrn_lb_logits.astype(jnp.float32), axis=0)
    lb_all = jnp.cumsum(lb_sm, axis=0) - lb_sm[0:1]
    for layer in range(DEPTH):
        h = rmsnorm(x, norm_mix_g[layer])
        j = layer // 2
        if layer % 2 == 0:
            mix = retention_nsa_mixer(h, even_w_in[j], even_w_out[j],
                                      cmp_pos_k[j], cmp_w1_k[j], cmp_w2_k[j],
                                      cmp_pos_v[j], cmp_w1_v[j], cmp_w2_v[j])
        else:
            mix = hgrn2_mixer(h, odd_w_in[j], odd_w_out[j], hgrn_norm_g[j], lb_all[j])
        x = x + mix.astype(x.dtype)
        h = rmsnorm(x, norm_ffn_g[layer])
        x = x + swiglu(h, ffn_w1[layer], ffn_w3[layer], ffn_w2[layer]).astype(x.dtype)
    return rmsnorm(x, final_norm_g)
```

```python
import functools
import math

import jax
import jax.numpy as jnp
from jax import lax
from jax.experimental import pallas as pl
from jax.experimental.pallas import tpu as pltpu

F32 = jnp.float32
BF16 = jnp.bfloat16

D_MODEL = 1024
DEPTH = 4
RET_HEADS = 4
RET_DK = 128
RET_DV = 128
ROPE_BASE = 10000.0
NSA_HEADS = 8
NSA_KV_GROUPS = 2
NSA_HPG = NSA_HEADS // NSA_KV_GROUPS
NSA_HD = 64
CMP_BLOCK = 32
CMP_STRIDE = 16
SLC_BLOCK = 64
N_SELECT = 8
FORCED_LOCAL = 2
FORCE_BONUS = 1.0e4
WINDOW = 512
Q_BLOCK = 128
HGRN_HEADS = 8
HGRN_DK = D_MODEL // HGRN_HEADS
HGRN_DV = D_MODEL // HGRN_HEADS
HGRN_CHUNK = 64
D_FF = -(-8 * D_MODEL // (3 * 256)) * 256
EPS = 1e-6
EVEN_MIX = RET_HEADS * RET_DV + NSA_HEADS * NSA_HD

LANES = 128
VMEM_LIMIT = 56 * 1024 * 1024
NEG = -1e30

RET_W = RET_HEADS * RET_DK
COL_RQ, COL_RK, COL_RV, COL_RG = 0, RET_W, 2 * RET_W, 3 * RET_W
COL_NQ = 4 * RET_W
NQ_W = NSA_HPG * NSA_HD
COL_SLC = COL_NQ + NSA_KV_GROUPS * NQ_W
COL_WIN = COL_SLC + NSA_KV_GROUPS * LANES
COL_GATE = COL_WIN + NSA_KV_GROUPS * LANES
EVEN_MAIN = COL_GATE + NSA_KV_GROUPS * LANES
EVEN_N = EVEN_MAIN + NSA_KV_GROUPS * LANES


def _dot(a, b):
    return jnp.dot(a, b, preferred_element_type=F32)


def _dot_nt(a, b):
    return lax.dot_general(a, b, (((1,), (1,)), ((), ())), preferred_element_type=F32)


def _dot_tn(a, b):
    return lax.dot_general(a, b, (((0,), (0,)), ((), ())), preferred_element_type=F32)


def _split3(x):
    hi = x.astype(BF16)
    r = x - hi.astype(F32)
    mid = r.astype(BF16)
    lo = (r - mid.astype(F32)).astype(BF16)
    return hi, mid, lo


def _dot_exact_lhs(sel, x):
    hi, mid, lo = _split3(x)
    return _dot(sel, hi) + _dot(sel, mid) + _dot(sel, lo)


def _dot_exact_rhs(x, sel):
    hi, mid, lo = _split3(x)
    return _dot(hi, sel) + _dot(mid, sel) + _dot(lo, sel)


def _rms(x, g):
    return x * lax.rsqrt(jnp.mean(x * x, axis=-1, keepdims=True) + EPS) * g


def _sigmoid(x):
    return 1.0 / (1.0 + jnp.exp(-x))


def _norm_proj_kernel(x_ref, g_ref, w_ref, *o_refs):
    h = _rms(x_ref[...], g_ref[...]).astype(BF16)
    off = 0
    for o_ref in o_refs:
        wd = o_ref.shape[1]
        o_ref[...] = _dot(h, w_ref[:, off:off + wd])
        off += wd


def _norm_proj(x2, g, w, widths, tm=512):
    T, D = x2.shape
    N = w.shape[1]
    assert sum(widths) == N and T % tm == 0
    return pl.pallas_call(
        _norm_proj_kernel,
        grid=(T // tm,),
        in_specs=[
            pl.BlockSpec((tm, D), lambda i: (i, 0)),
            pl.BlockSpec((1, D), lambda i: (0, 0)),
            pl.BlockSpec((D, N), lambda i: (0, 0)),
        ],
        out_specs=[pl.BlockSpec((tm, wd), lambda i: (i, 0)) for wd in widths],
        out_shape=[jax.ShapeDtypeStruct((T, wd), F32) for wd in widths],
        compiler_params=pltpu.CompilerParams(
            dimension_semantics=("parallel",), vmem_limit_bytes=VMEM_LIMIT),
        name="norm_proj",
    )(x2, g.reshape(1, D), w)


def _retention_kernel(q_ref, k_ref, v_ref, g_ref, cos_ref, sin_ref, o_ref, state_ref, *, cb):
    @pl.when(pl.program_id(1) == 0)
    def _():
        state_ref[...] = jnp.zeros_like(state_ref)

    cosf = cos_ref[...]
    sinf = sin_ref[...]
    ri = lax.broadcasted_iota(jnp.int32, (cb, cb), 0)
    ci = lax.broadcasted_iota(jnp.int32, (cb, cb), 1)
    diff = (ri - ci).astype(F32)
    causal = ri >= ci
    idx = lax.broadcasted_iota(jnp.int32, (cb, 1), 0).astype(F32)
    for h in range(RET_HEADS):
        lg = math.log(1.0 - 2.0 ** (-5.0 - h))
        sl = slice(h * RET_DK, (h + 1) * RET_DK)
        q = q_ref[:, sl]
        k = k_ref[:, sl]
        v = v_ref[:, sl].astype(BF16)
        qr = q * cosf + pltpu.roll(q, RET_DK // 2, 1) * sinf
        kr = (k * cosf + pltpu.roll(k, RET_DK // 2, 1) * sinf) * (RET_DK ** -0.5)
        dmat = jnp.where(causal, jnp.exp(lg * jnp.where(causal, diff, 0.0)), 0.0)
        scores = _dot_nt(qr.astype(BF16), kr.astype(BF16)) * dmat
        o = _dot(scores.astype(BF16), v)
        q_dec = qr * jnp.exp(lg * (idx + 1.0))
        state = state_ref[h]
        o = o + _dot(q_dec.astype(BF16), state.astype(BF16))
        k_dec = kr * jnp.exp(lg * (cb - 1.0 - idx))
        state_ref[h] = math.exp(lg * cb) * state + _dot_tn(k_dec.astype(BF16), v)
        mu = jnp.mean(o, axis=-1, keepdims=True)
        oc = o - mu
        var = jnp.mean(oc * oc, axis=-1, keepdims=True)
        gate = g_ref[:, sl]
        o_ref[:, sl] = (oc * lax.rsqrt(var + 1e-5) * (gate * _sigmoid(gate))).astype(o_ref.dtype)


def _retention(main, cosf, sinf, B, S, cb=256):
    T = B * S
    nc = S // cb
    blk = lambda c: pl.BlockSpec((cb, RET_W), lambda b, i, c=c: (b * nc + i, c))
    tab = pl.BlockSpec((cb, RET_DK), lambda b, i: (i, 0))
    return pl.pallas_call(
        functools.partial(_retention_kernel, cb=cb),
        grid=(B, nc),
        in_specs=[blk(COL_RQ // RET_W), blk(COL_RK // RET_W), blk(COL_RV // RET_W),
                  blk(COL_RG // RET_W), tab, tab],
        out_specs=pl.BlockSpec((cb, RET_W), lambda b, i: (b * nc + i, 0)),
        out_shape=jax.ShapeDtypeStruct((T, RET_W), BF16),
        scratch_shapes=[pltpu.VMEM((RET_HEADS, RET_DK, RET_DV), F32)],
        compiler_params=pltpu.CompilerParams(
            dimension_semantics=("parallel", "arbitrary"), vmem_limit_bytes=VMEM_LIMIT),
        name="retention",
    )(main, main, main, main, cosf, sinf)


def _gelu_tanh(y):
    return 0.5 * y * (1.0 + jnp.tanh(math.sqrt(2.0 / math.pi) * (y + 0.044715 * (y * y * y))))


def _compress_kernel(r0_ref, r1_ref, pos_ref, w1_ref, w2_ref, o_ref):
    for g, r_ref in enumerate((r0_ref, r1_ref)):
        r = r_ref[0]
        n = r.shape[0]
        y_lo = _dot((r + pos_ref[0]).astype(BF16), w1_ref[0])
        y_hi = _dot((r + pos_ref[1]).astype(BF16), w1_ref[1])
        y = y_lo + pltpu.roll(y_hi, n - 1, 0)
        out = _dot(_gelu_tanh(y).astype(BF16), w2_ref[...])
        row = lax.broadcasted_iota(jnp.int32, out.shape, 0)
        o_ref[0, g] = jnp.where(row < n - 1, out, 0.0)


def _compress(cmp0, cmp1, pos, w1, w2, B, S):
    nr = S // CMP_STRIDE
    kw = CMP_STRIDE * LANES
    r0 = cmp0.reshape(B, nr, kw)
    r1 = cmp1.reshape(B, nr, kw)
    rspec = pl.BlockSpec((1, nr, kw), lambda b: (b, 0, 0))
    return pl.pallas_call(
        _compress_kernel,
        grid=(B,),
        in_specs=[rspec, rspec,
                  pl.BlockSpec((2, 1, kw), lambda b: (0, 0, 0)),
                  pl.BlockSpec((2, kw, LANES), lambda b: (0, 0, 0)),
                  pl.BlockSpec((LANES, LANES), lambda b: (0, 0))],
        out_specs=pl.BlockSpec((1, NSA_KV_GROUPS, nr, LANES), lambda b: (b, 0, 0, 0)),
        out_shape=jax.ShapeDtypeStruct((B, NSA_KV_GROUPS, nr, LANES), F32),
        compiler_params=pltpu.CompilerParams(
            dimension_semantics=("parallel",), vmem_limit_bytes=VMEM_LIMIT),
        name="nsa_compress",
    )(r0, r1, pos, w1, w2)


def _nsa_kernel(q_ref, cmp_ref, slc_ref, win_ref, gate_ref, o_ref,
                ks_ref, vs_ref, kw_ref, vw_ref, m_ref, l_ref, acc_ref, *, S, kc):
    qb = pl.program_id(2)
    q0 = qb * Q_BLOCK
    n_slc = S // SLC_BLOCK
    n_cmp = S // CMP_STRIDE
    hd = NSA_HD

    @pl.when(qb == 0)
    def _():
        slab = slc_ref[...]
        ks_ref[...] = slab[:, :hd].astype(BF16)
        vs_ref[...] = slab[:, hd:].astype(BF16)
        slab = win_ref[...]
        kw_ref[...] = slab[:, :hd].astype(BF16)
        vw_ref[...] = slab[:, hd:].astype(BF16)

    q_all = (q_ref[...] * (hd ** -0.5)).astype(BF16)
    qh = [q_all[:, h * hd:(h + 1) * hd] for h in range(NSA_HPG)]
    t_col = q0 + lax.broadcasted_iota(jnp.int32, (Q_BLOCK, 1), 0)

    k_cmp = cmp_ref[0, 0, :, :hd].astype(BF16)
    v_cmp = cmp_ref[0, 0, :, hd:].astype(BF16)
    cmp_end = lax.broadcasted_iota(jnp.int32, (1, n_cmp), 1) * CMP_STRIDE + (CMP_BLOCK - 1)
    mask_c = cmp_end <= t_col
    o_c = []
    p_sum = jnp.zeros((Q_BLOCK, n_cmp), F32)
    for h in range(NSA_HPG):
        s = jnp.where(mask_c, _dot_nt(qh[h], k_cmp), NEG)
        m = jnp.max(s, axis=-1, keepdims=True)
        e = jnp.where(mask_c, jnp.exp(s - m), 0.0)
        p = e / jnp.maximum(jnp.sum(e, axis=-1, keepdims=True), 1e-30)
        o_c.append(_dot(p.astype(BF16), v_cmp))
        p_sum = p_sum + p

    cn = lax.broadcasted_iota(jnp.int32, (n_cmp, n_slc), 0) * CMP_STRIDE
    sj = lax.broadcasted_iota(jnp.int32, (n_cmp, n_slc), 1) * SLC_BLOCK
    overlap = jnp.where(cn <= sj + (SLC_BLOCK - 1),
                        jnp.where(cn + (CMP_BLOCK - 1) >= sj, 1.0, 0.0), 0.0).astype(BF16)
    imp_t = _dot_exact_rhs(p_sum, overlap).T
    jblk = lax.broadcasted_iota(jnp.int32, (n_slc, Q_BLOCK), 0)
    bt = (q0 + lax.broadcasted_iota(jnp.int32, (n_slc, Q_BLOCK), 1)) // SLC_BLOCK
    back = bt - jblk
    bonus = jnp.where(jblk == 0, FORCE_BONUS,
                      jnp.where(back >= 0, jnp.where(back < FORCED_LOCAL, FORCE_BONUS, 0.0), 0.0))
    score = jnp.where(jblk <= bt, imp_t + bonus, NEG)
    rank = jnp.zeros((n_slc, Q_BLOCK), F32)
    for i in range(n_slc):
        row = score[i:i + 1, :]
        ge = jnp.where(row >= score, 1.0, 0.0)
        gt = jnp.where(row > score, 1.0, 0.0)
        rank = rank + jnp.where(jblk > i, ge, gt)
    sel = jnp.where(rank < float(min(N_SELECT, n_slc)), 1.0, 0.0).T.astype(BF16)

    m_ref[...] = jnp.full_like(m_ref, NEG)
    l_ref[...] = jnp.zeros_like(l_ref)
    acc_ref[...] = jnp.zeros_like(acc_ref)
    bpc = kc // SLC_BLOCK
    e_row = lax.broadcasted_iota(jnp.int32, (n_slc, kc), 0)
    e_col = lax.broadcasted_iota(jnp.int32, (n_slc, kc), 1) // SLC_BLOCK
    k_lane = lax.broadcasted_iota(jnp.int32, (1, kc), 1)

    def chunk(c, carry):
        expand = jnp.where(e_row == e_col + c * bpc, 1.0, 0.0).astype(BF16)
        picked = _dot(sel, expand)
        keep = jnp.where(k_lane + c * kc <= t_col, picked, 0.0) > 0.5
        r0 = pl.multiple_of(c * kc, kc)
        kblk = ks_ref[pl.ds(r0, kc), :]
        vblk = vs_ref[pl.ds(r0, kc), :]
        for h in range(NSA_HPG):
            s = jnp.where(keep, _dot_nt(qh[h], kblk), NEG)
            m_old = m_ref[h]
            m_new = jnp.maximum(m_old, jnp.max(s, axis=-1, keepdims=True))
            alpha = jnp.exp(m_old - m_new)
            p = jnp.where(keep, jnp.exp(s - m_new), 0.0)
            l_ref[h] = alpha * l_ref[h] + jnp.sum(p, axis=-1, keepdims=True)
            acc_ref[h] = alpha * acc_ref[h] + _dot(p.astype(BF16), vblk)
            m_ref[h] = m_new
        return carry

    lax.fori_loop(0, (q0 + Q_BLOCK - 1) // kc + 1, chunk, 0)

    wlen = WINDOW + Q_BLOCK
    w0 = pl.multiple_of(jnp.maximum(q0 - WINDOW, 0), Q_BLOCK)
    kwin = kw_ref[pl.ds(w0, wlen), :]
    vwin = vw_ref[pl.ds(w0, wlen), :]
    dist = t_col - (w0 + lax.broadcasted_iota(jnp.int32, (1, wlen), 1))
    mask_w = (dist >= 0) & (dist < WINDOW)

    sig = _sigmoid(gate_ref[...])
    outs = []
    for h in range(NSA_HPG):
        s = jnp.where(mask_w, _dot_nt(qh[h], kwin), NEG)
        m = jnp.max(s, axis=-1, keepdims=True)
        e = jnp.where(mask_w, jnp.exp(s - m), 0.0)
        p = e / jnp.maximum(jnp.sum(e, axis=-1, keepdims=True), 1e-30)
        o_w = _dot(p.astype(BF16), vwin)
        o_s = acc_ref[h] / jnp.maximum(l_ref[h], 1e-30)
        outs.append(sig[:, h:h + 1] * o_c[h]
                    + sig[:, NSA_HPG + h:NSA_HPG + h + 1] * o_s
                    + sig[:, 2 * NSA_HPG + h:2 * NSA_HPG + h + 1] * o_w)
    o_ref[...] = jnp.concatenate(outs, axis=-1).astype(o_ref.dtype)


def _nsa(main, kvcmp, B, S, kc=512):
    T = B * S
    nqb = S // Q_BLOCK
    G = NSA_KV_GROUPS
    assert S % kc == 0 and S >= WINDOW + Q_BLOCK
    return pl.pallas_call(
        functools.partial(_nsa_kernel, S=S, kc=kc),
        grid=(B, G, nqb),
        in_specs=[
            pl.BlockSpec((Q_BLOCK, NQ_W), lambda b, g, i: (b * nqb + i, COL_NQ // NQ_W + g)),
            pl.BlockSpec((1, 1, S // CMP_STRIDE, LANES), lambda b, g, i: (b, g, 0, 0)),
            pl.BlockSpec((S, LANES), lambda b, g, i: (b, COL_SLC // LANES + g)),
            pl.BlockSpec((S, LANES), lambda b, g, i: (b, COL_WIN // LANES + g)),
            pl.BlockSpec((Q_BLOCK, LANES), lambda b, g, i: (b * nqb + i, COL_GATE // LANES + g)),
        ],
        out_specs=pl.BlockSpec((Q_BLOCK, NQ_W), lambda b, g, i: (b * nqb + i, g)),
        out_shape=jax.ShapeDtypeStruct((T, G * NQ_W), BF16),
        scratch_shapes=[pltpu.VMEM((S, NSA_HD), BF16) for _ in range(4)] + [
            pltpu.VMEM((NSA_HPG, Q_BLOCK, 1), F32),
            pltpu.VMEM((NSA_HPG, Q_BLOCK, 1), F32),
            pltpu.VMEM((NSA_HPG, Q_BLOCK, NSA_HD), F32)],
        compiler_params=pltpu.CompilerParams(
            dimension_semantics=("parallel", "parallel", "arbitrary"), vmem_limit_bytes=VMEM_LIMIT),
        name="nsa_attention",
    )(main, kvcmp, main, main, main)


HGRN_SUB = HGRN_CHUNK // 2
HGRN_SAFE_RANGE = 60.0


def _hgrn_kernel(q_ref, f_ref, i_ref, g_ref, lbl_ref, ng_ref, o_ref,
                 st_ref, q_s, k_s, b_s, v_s, oi_s, *, rb, layer_j):
    C, SB = HGRN_CHUNK, HGRN_SUB
    nch = rb // C

    @pl.when(pl.program_id(2) == 0)
    def _():
        st_ref[...] = jnp.zeros_like(st_ref)

    logits = lbl_ref[...]
    e = jnp.exp(logits - jnp.max(logits, axis=0, keepdims=True))
    sm = e / jnp.sum(e, axis=0, keepdims=True)
    lb = jnp.sum(sm[:layer_j + 1], axis=0, keepdims=True) - sm[0:1]
    lb = jnp.maximum(lb, 0.0)

    fp = f_ref[...]
    qp = q_ref[...]
    log_sig = jnp.minimum(fp, 0.0) - jnp.log1p(jnp.exp(-jnp.abs(fp)))
    c_term = jnp.log1p(-lb) + log_sig
    a_term = jnp.log(lb)
    log_f = jnp.maximum(a_term, c_term) + jnp.log1p(jnp.exp(-jnp.abs(a_term - c_term)))
    k = (1.0 - lb) * _sigmoid(-fp)
    q = qp * _sigmoid(qp) * (HGRN_DK ** -0.5)
    v = i_ref[...]

    ri = lax.broadcasted_iota(jnp.int32, (rb, rb), 0)
    ci = lax.broadcasted_iota(jnp.int32, (rb, rb), 1)
    same_chunk = (ri // C) == (ci // C)
    lower = same_chunk & (ci <= ri)
    b = _dot_exact_lhs(jnp.where(lower, 1.0, 0.0).astype(BF16), log_f)

    span = jnp.zeros((1, HGRN_DK), F32)
    for s0 in range(0, rb, SB):
        span = jnp.maximum(span, b[s0:s0 + 1] - b[s0 + SB - 1:s0 + SB])
    safe = jnp.max(span) <= HGRN_SAFE_RANGE

    q_s[...] = q
    k_s[...] = k
    b_s[...] = b
    v_s[...] = v

    @pl.when(safe)
    def _():
        qd, kd, qo, ko = [], [], [], []
        for s0 in range(0, rb, SB):
            bb = b[s0:s0 + SB]
            ref = b[s0 + SB // 2:s0 + SB // 2 + 1]
            qd.append(q[s0:s0 + SB] * jnp.exp(bb - ref))
            kd.append(k[s0:s0 + SB] * jnp.exp(ref - bb))
            if (s0 // SB) % 2 == 0:
                end = b[s0 + SB - 1:s0 + SB]
                ko.append(k[s0:s0 + SB] * jnp.exp(end - bb))
                qo.append(jnp.zeros((SB, HGRN_DK), F32))
            else:
                end = b[s0 - 1:s0]
                qo.append(q[s0:s0 + SB] * jnp.exp(bb - end))
                ko.append(jnp.zeros((SB, HGRN_DK), F32))
        cat = lambda xs: jnp.concatenate(xs, axis=0).astype(BF16)
        a_diag = _dot_nt(cat(qd), cat(kd))
        a_off = _dot_nt(cat(qo), cat(ko))
        same_sub = (ri // SB) == (ci // SB)
        attn = jnp.where(lower, jnp.where(same_sub, a_diag, a_off), 0.0)
        oi_s[...] = _dot(attn.astype(BF16), v.astype(BF16))

    @pl.when(jnp.logical_not(safe))
    def _():
        def row(n, carry):
            c0 = pl.multiple_of((n // C) * C, C)
            qn = q_s[pl.ds(n, 1), :]
            bn = b_s[pl.ds(n, 1), :]
            kb = k_s[pl.ds(c0, C), :]
            bb = b_s[pl.ds(c0, C), :]
            vb = v_s[pl.ds(c0, C), :]
            rows = c0 + lax.broadcasted_iota(jnp.int32, (C, 1), 0)
            decay = jnp.exp(jnp.where(rows <= n, bn - bb, -jnp.inf))
            a = jnp.sum(qn * decay * kb, axis=-1, keepdims=True)
            oi_s[pl.ds(n, 1), :] = jnp.sum(a * vb, axis=0, keepdims=True)
            return carry
        lax.fori_loop(0, rb, row, 0)

    qb = (q * jnp.exp(b)).astype(BF16)
    vb16 = v.astype(BF16)
    st = st_ref[...]
    outs = []
    for c in range(nch):
        sl = slice(c * C, (c + 1) * C)
        b_last = b[(c + 1) * C - 1:(c + 1) * C]
        outs.append(oi_s[sl, :] + _dot_nt(qb[sl], st.astype(BF16)))
        k_dec = (k[sl] * jnp.exp(b_last - b[sl])).astype(BF16)
        st = jnp.exp(b_last) * st + _dot_tn(vb16[sl], k_dec)
    st_ref[...] = st
    o = jnp.concatenate(outs, axis=0)
    gp = g_ref[...]
    o_ref[...] = (_rms(o, ng_ref[...]) * _sigmoid(gp)).astype(o_ref.dtype)


def _hgrn(proj, lb_logits, norm_g, layer_j, B, S, rb=256):
    T = B * S
    nrb = S // rb
    H = HGRN_HEADS
    n_layers = lb_logits.shape[0]
    blk = lambda c: pl.BlockSpec((rb, HGRN_DK), lambda b, h, r, c=c: (b * nrb + r, c * H + h))
    vm = lambda: pltpu.VMEM((rb, HGRN_DK), F32)
    return pl.pallas_call(
        functools.partial(_hgrn_kernel, rb=rb, layer_j=layer_j),
        grid=(B, H, nrb),
        in_specs=[blk(0), blk(1), blk(2), blk(3),
                  pl.BlockSpec((n_layers, HGRN_DK), lambda b, h, r: (0, h)),
                  pl.BlockSpec((1, HGRN_DV), lambda b, h, r: (0, 0))],
        out_specs=pl.BlockSpec((rb, HGRN_DV), lambda b, h, r: (b * nrb + r, h)),
        out_shape=jax.ShapeDtypeStruct((T, H * HGRN_DV), BF16),
        scratch_shapes=[pltpu.VMEM((HGRN_DV, HGRN_DK), F32), vm(), vm(), vm(), vm(), vm()],
        compiler_params=pltpu.CompilerParams(
            dimension_semantics=("parallel", "parallel", "arbitrary"), vmem_limit_bytes=VMEM_LIMIT),
        name="hgrn2",
    )(proj, proj, proj, proj, lb_logits, norm_g.reshape(1, HGRN_DV))


def _mix_ffn_kernel(*refs, n_mix):
    x_ref = refs[0]
    mix_refs = refs[1:1 + n_mix]
    wo_refs = refs[1 + n_mix:1 + 2 * n_mix]
    g_ref, w1_ref, w3_ref, w2_ref, o_ref, x1_s, h_s, acc_s = refs[1 + 2 * n_mix:]
    f = pl.program_id(1)

    @pl.when(f == 0)
    def _():
        x1 = x_ref[...]
        for m_ref, wo_ref in zip(mix_refs, wo_refs):
            x1 = x1 + _dot(m_ref[...], wo_ref[...])
        x1_s[...] = x1
        h_s[...] = _rms(x1, g_ref[...]).astype(BF16)

    h = h_s[...]
    a = _dot(h, w1_ref[...])
    u = (a * _sigmoid(a) * _dot(h, w3_ref[...])).astype(BF16)
    part = _dot(u, w2_ref[...])

    @pl.when(f == 0)
    def _():
        acc_s[...] = part

    @pl.when(f > 0)
    def _():
        acc_s[...] += part

    @pl.when(f == pl.num_programs(1) - 1)
    def _():
        o_ref[...] = x1_s[...] + acc_s[...]


def _mix_ffn(x2, mixes, wos, g, w1, w3, w2, tm=1024, tf=256):
    T, D = x2.shape
    F = w1.shape[1]
    assert T % tm == 0 and F % tf == 0
    n_mix = len(mixes)
    in_specs = [pl.BlockSpec((tm, D), lambda i, f: (i, 0))]
    in_specs += [pl.BlockSpec((tm, m.shape[1]), lambda i, f: (i, 0)) for m in mixes]
    in_specs += [pl.BlockSpec(w.shape, lambda i, f: (0, 0)) for w in wos]
    in_specs += [pl.BlockSpec((1, D), lambda i, f: (0, 0)),
                 pl.BlockSpec((D, tf), lambda i, f: (0, f)),
                 pl.BlockSpec((D, tf), lambda i, f: (0, f)),
                 pl.BlockSpec((tf, D), lambda i, f: (f, 0))]
    return pl.pallas_call(
        functools.partial(_mix_ffn_kernel, n_mix=n_mix),
        grid=(T // tm, F // tf),
        in_specs=in_specs,
        out_specs=pl.BlockSpec((tm, D), lambda i, f: (i, 0)),
        out_shape=jax.ShapeDtypeStruct((T, D), F32),
        scratch_shapes=[pltpu.VMEM((tm, D), F32), pltpu.VMEM((tm, D), BF16), pltpu.VMEM((tm, D), F32)],
        compiler_params=pltpu.CompilerParams(
            dimension_semantics=("parallel", "arbitrary"), vmem_limit_bytes=VMEM_LIMIT),
        name="mix_ffn",
    )(x2, *mixes, *wos, g.reshape(1, D), w1, w3, w2)


def _final_norm_kernel(x_ref, g_ref, o_ref):
    o_ref[...] = _rms(x_ref[...], g_ref[...])


def _final_norm(x2, g, tm=1024):
    T, D = x2.shape
    return pl.pallas_call(
        _final_norm_kernel,
        grid=(T // tm,),
        in_specs=[pl.BlockSpec((tm, D), lambda i: (i, 0)), pl.BlockSpec((1, D), lambda i: (0, 0))],
        out_specs=pl.BlockSpec((tm, D), lambda i: (i, 0)),
        out_shape=jax.ShapeDtypeStruct((T, D), F32),
        compiler_params=pltpu.CompilerParams(
            dimension_semantics=("parallel",), vmem_limit_bytes=VMEM_LIMIT),
        name="final_norm",
    )(x2, g.reshape(1, D))


def _even_w_in_layout(w):
    D = w.shape[0]
    hd, G = NSA_HD, NSA_KV_GROUPS
    o_kc = 4 * RET_W + NSA_HEADS * hd
    kv = lambda idx, g: w[:, o_kc + idx * G * hd + g * hd:o_kc + idx * G * hd + (g + 1) * hd]
    o_ng = o_kc + 6 * G * hd
    cols = [w[:, :o_kc]]
    for idx_k, idx_v in ((2, 3), (4, 5)):
        for g in range(G):
            cols += [kv(idx_k, g), kv(idx_v, g)]
    for g in range(G):
        gate = [w[:, o_ng + c * NSA_HEADS + g * NSA_HPG:o_ng + c * NSA_HEADS + (g + 1) * NSA_HPG]
                for c in range(3)]
        cols += gate + [jnp.zeros((D, LANES - 3 * NSA_HPG), w.dtype)]
    for g in range(G):
        cols += [kv(0, g), kv(1, g)]
    out = jnp.concatenate(cols, axis=1)
    assert out.shape[1] == EVEN_N
    return out.astype(BF16)


def _compress_weights(pos_k, w1_k, w2_k, pos_v, w1_v, w2_v):
    hd = NSA_HD
    z = jnp.zeros((CMP_BLOCK, hd, hd), F32)
    w1k = w1_k.reshape(CMP_BLOCK, hd, hd)
    w1v = w1_v.reshape(CMP_BLOCK, hd, hd)
    top = jnp.concatenate([w1k, z], axis=2)
    bot = jnp.concatenate([z, w1v], axis=2)
    w1 = jnp.concatenate([top, bot], axis=1)
    w1 = w1.reshape(2, CMP_STRIDE * LANES, LANES).astype(BF16)
    pos = jnp.concatenate([pos_k, pos_v], axis=1).reshape(2, 1, CMP_STRIDE * LANES)
    zz = jnp.zeros((hd, hd), F32)
    w2 = jnp.concatenate([jnp.concatenate([w2_k, zz], axis=1),
                          jnp.concatenate([zz, w2_v], axis=1)], axis=0).astype(BF16)
    return pos, w1, w2


def _rotary_tables(S):
    half = RET_DK // 2
    inv = ROPE_BASE ** (-jnp.arange(half, dtype=F32) / half)
    ang = jnp.arange(S, dtype=F32)[:, None] * inv[None, :]
    cos, sin = jnp.cos(ang), jnp.sin(ang)
    return jnp.concatenate([cos, cos], axis=1), jnp.concatenate([-sin, sin], axis=1)


def kernel(x, norm_mix_g, norm_ffn_g, final_norm_g, even_w_in, even_w_out, cmp_pos_k, cmp_w1_k, cmp_w2_k, cmp_pos_v, cmp_w1_v, cmp_w2_v, odd_w_in, odd_w_out, hgrn_norm_g, hgrn_lb_logits, ffn_w1, ffn_w3, ffn_w2):
    B, S, D = x.shape
    x2 = x.reshape(B * S, D)
    cosf, sinf = _rotary_tables(S)
    for layer in range(DEPTH):
        j = layer // 2
        if layer % 2 == 0:
            w_in = _even_w_in_layout(even_w_in[j])
            main, cmp0, cmp1 = _norm_proj(x2, norm_mix_g[layer], w_in, (EVEN_MAIN, LANES, LANES))
            pos, cw1, cw2 = _compress_weights(cmp_pos_k[j], cmp_w1_k[j], cmp_w2_k[j],
                                              cmp_pos_v[j], cmp_w1_v[j], cmp_w2_v[j])
            kvcmp = _compress(cmp0, cmp1, pos, cw1, cw2, B, S)
            o_r = _retention(main, cosf, sinf, B, S)
            o_n = _nsa(main, kvcmp, B, S)
            w_out = even_w_out[j].astype(BF16)
            mixes, wos = (o_r, o_n), (w_out[:RET_W], w_out[RET_W:])
        else:
            (proj,) = _norm_proj(x2, norm_mix_g[layer], odd_w_in[j].astype(BF16), (4 * D,))
            o_h = _hgrn(proj, hgrn_lb_logits, hgrn_norm_g[j], j, B, S)
            mixes, wos = (o_h,), (odd_w_out[j].astype(BF16),)
        x2 = _mix_ffn(x2, mixes, wos, norm_ffn_g[layer],
                      ffn_w1[layer].astype(BF16), ffn_w3[layer].astype(BF16), ffn_w2[layer].astype(BF16))
    return _final_norm(x2, final_norm_g).reshape(B, S, D)
```

```python
import functools
import math

import jax
import jax.numpy as jnp
from jax import lax
from jax.experimental import pallas as pl
from jax.experimental.pallas import tpu as pltpu

F32 = jnp.float32
BF16 = jnp.bfloat16

D_MODEL = 1024
DEPTH = 4
RET_HEADS = 4
RET_DK = 128
RET_DV = 128
ROPE_BASE = 10000.0
NSA_HEADS = 8
NSA_KV_GROUPS = 2
NSA_HPG = NSA_HEADS // NSA_KV_GROUPS
NSA_HD = 64
CMP_BLOCK = 32
CMP_STRIDE = 16
SLC_BLOCK = 64
N_SELECT = 8
FORCED_LOCAL = 2
FORCE_BONUS = 1.0e4
WINDOW = 512
Q_BLOCK = 128
HGRN_HEADS = 8
HGRN_DK = D_MODEL // HGRN_HEADS
HGRN_DV = D_MODEL // HGRN_HEADS
HGRN_CHUNK = 64
D_FF = -(-8 * D_MODEL // (3 * 256)) * 256
EPS = 1e-6
EVEN_MIX = RET_HEADS * RET_DV + NSA_HEADS * NSA_HD

LANES = 128
VMEM_LIMIT = 56 * 1024 * 1024
NEG = -1e30
LOG2E = math.log2(math.e)

RET_W = RET_HEADS * RET_DK
COL_RQ, COL_RK, COL_RV, COL_RG = 0, RET_W, 2 * RET_W, 3 * RET_W
COL_NQ = 4 * RET_W
NQ_W = NSA_HPG * NSA_HD
COL_SLC = COL_NQ + NSA_KV_GROUPS * NQ_W
COL_WIN = COL_SLC + NSA_KV_GROUPS * LANES
COL_GATE = COL_WIN + NSA_KV_GROUPS * LANES
EVEN_MAIN = COL_GATE + NSA_KV_GROUPS * LANES
EVEN_N = EVEN_MAIN + NSA_KV_GROUPS * LANES


def _dot(a, b):
    return jnp.dot(a, b, preferred_element_type=F32)


def _dot_nt(a, b):
    return lax.dot_general(a, b, (((1,), (1,)), ((), ())), preferred_element_type=F32)


def _dot_tn(a, b):
    return lax.dot_general(a, b, (((0,), (0,)), ((), ())), preferred_element_type=F32)


def _split3(x):
    hi = x.astype(BF16)
    r = x - hi.astype(F32)
    mid = r.astype(BF16)
    lo = (r - mid.astype(F32)).astype(BF16)
    return hi, mid, lo


def _dot_exact_lhs(sel, x):
    hi, mid, lo = _split3(x)
    return _dot(sel, hi) + _dot(sel, mid) + _dot(sel, lo)


def _dot_exact_rhs(x, sel):
    hi, mid, lo = _split3(x)
    return _dot(hi, sel) + _dot(mid, sel) + _dot(lo, sel)


def _rms(x, g):
    return x * lax.rsqrt(jnp.mean(x * x, axis=-1, keepdims=True) + EPS) * g


def _sigmoid(x):
    return 1.0 / (1.0 + jnp.exp(-x))


def _norm_proj_kernel(x_ref, g_ref, w_ref, *o_refs):
    h = _rms(x_ref[...], g_ref[...]).astype(BF16)
    off = 0
    for o_ref in o_refs:
        wd = o_ref.shape[1]
        o_ref[...] = _dot(h, w_ref[:, off:off + wd])
        off += wd


def _norm_proj(x2, g, w, widths, tm=512):
    T, D = x2.shape
    N = w.shape[1]
    assert sum(widths) == N and T % tm == 0
    return pl.pallas_call(
        _norm_proj_kernel,
        grid=(T // tm,),
        in_specs=[
            pl.BlockSpec((tm, D), lambda i: (i, 0)),
            pl.BlockSpec((1, D), lambda i: (0, 0)),
            pl.BlockSpec((D, N), lambda i: (0, 0)),
        ],
        out_specs=[pl.BlockSpec((tm, wd), lambda i: (i, 0)) for wd in widths],
        out_shape=[jax.ShapeDtypeStruct((T, wd), F32) for wd in widths],
        compiler_params=pltpu.CompilerParams(
            dimension_semantics=("parallel",), vmem_limit_bytes=VMEM_LIMIT),
        name="norm_proj",
    )(x2, g.reshape(1, D), w)


def _retention_kernel(q_ref, k_ref, v_ref, g_ref, cos_ref, sin_ref, o_ref, state_ref, *, cb):
    @pl.when(pl.program_id(1) == 0)
    def _():
        state_ref[...] = jnp.zeros_like(state_ref)

    cosf = cos_ref[...]
    sinf = sin_ref[...]
    ri = lax.broadcasted_iota(jnp.int32, (cb, cb), 0)
    ci = lax.broadcasted_iota(jnp.int32, (cb, cb), 1)
    diff = (ri - ci).astype(F32)
    causal = ri >= ci
    idx = lax.broadcasted_iota(jnp.int32, (cb, 1), 0).astype(F32)
    for h in range(RET_HEADS):
        lg = math.log(1.0 - 2.0 ** (-5.0 - h))
        sl = slice(h * RET_DK, (h + 1) * RET_DK)
        q = q_ref[:, sl]
        k = k_ref[:, sl]
        v = v_ref[:, sl].astype(BF16)
        qr = q * cosf + pltpu.roll(q, RET_DK // 2, 1) * sinf
        kr = (k * cosf + pltpu.roll(k, RET_DK // 2, 1) * sinf) * (RET_DK ** -0.5)
        dmat = jnp.where(causal, jnp.exp(lg * jnp.where(causal, diff, 0.0)), 0.0)
        scores = _dot_nt(qr.astype(BF16), kr.astype(BF16)) * dmat
        o = _dot(scores.astype(BF16), v)
        q_dec = qr * jnp.exp(lg * (idx + 1.0))
        state = state_ref[h]
        o = o + _dot(q_dec.astype(BF16), state.astype(BF16))
        k_dec = kr * jnp.exp(lg * (cb - 1.0 - idx))
        state_ref[h] = math.exp(lg * cb) * state + _dot_tn(k_dec.astype(BF16), v)
        mu = jnp.mean(o, axis=-1, keepdims=True)
        oc = o - mu
        var = jnp.mean(oc * oc, axis=-1, keepdims=True)
        gate = g_ref[:, sl]
        o_ref[:, sl] = (oc * lax.rsqrt(var + 1e-5) * (gate * _sigmoid(gate))).astype(o_ref.dtype)


def _retention(main, cosf, sinf, B, S, cb=256):
    T = B * S
    nc = S // cb
    blk = lambda c: pl.BlockSpec((cb, RET_W), lambda b, i, c=c: (b * nc + i, c))
    tab = pl.BlockSpec((cb, RET_DK), lambda b, i: (i, 0))
    return pl.pallas_call(
        functools.partial(_retention_kernel, cb=cb),
        grid=(B, nc),
        in_specs=[blk(COL_RQ // RET_W), blk(COL_RK // RET_W), blk(COL_RV // RET_W),
                  blk(COL_RG // RET_W), tab, tab],
        out_specs=pl.BlockSpec((cb, RET_W), lambda b, i: (b * nc + i, 0)),
        out_shape=jax.ShapeDtypeStruct((T, RET_W), BF16),
        scratch_shapes=[pltpu.VMEM((RET_HEADS, RET_DK, RET_DV), F32)],
        compiler_params=pltpu.CompilerParams(
            dimension_semantics=("parallel", "arbitrary"), vmem_limit_bytes=VMEM_LIMIT),
        name="retention",
    )(main, main, main, main, cosf, sinf)


def _gelu_tanh(y):
    return 0.5 * y * (1.0 + jnp.tanh(math.sqrt(2.0 / math.pi) * (y + 0.044715 * (y * y * y))))


def _compress_kernel(r0_ref, r1_ref, pos_ref, w1_ref, w2_ref, o_ref):
    for g, r_ref in enumerate((r0_ref, r1_ref)):
        r = r_ref[0]
        n = r.shape[0]
        y_lo = _dot((r + pos_ref[0]).astype(BF16), w1_ref[0])
        y_hi = _dot((r + pos_ref[1]).astype(BF16), w1_ref[1])
        y = y_lo + pltpu.roll(y_hi, n - 1, 0)
        out = _dot(_gelu_tanh(y).astype(BF16), w2_ref[...])
        row = lax.broadcasted_iota(jnp.int32, out.shape, 0)
        o_ref[0, g] = jnp.where(row < n - 1, out, 0.0)


def _compress(cmp0, cmp1, pos, w1, w2, B, S):
    nr = S // CMP_STRIDE
    kw = CMP_STRIDE * LANES
    r0 = cmp0.reshape(B, nr, kw)
    r1 = cmp1.reshape(B, nr, kw)
    rspec = pl.BlockSpec((1, nr, kw), lambda b: (b, 0, 0))
    return pl.pallas_call(
        _compress_kernel,
        grid=(B,),
        in_specs=[rspec, rspec,
                  pl.BlockSpec((2, 1, kw), lambda b: (0, 0, 0)),
                  pl.BlockSpec((2, kw, LANES), lambda b: (0, 0, 0)),
                  pl.BlockSpec((LANES, LANES), lambda b: (0, 0))],
        out_specs=pl.BlockSpec((1, NSA_KV_GROUPS, nr, LANES), lambda b: (b, 0, 0, 0)),
        out_shape=jax.ShapeDtypeStruct((B, NSA_KV_GROUPS, nr, LANES), F32),
        compiler_params=pltpu.CompilerParams(
            dimension_semantics=("parallel",), vmem_limit_bytes=VMEM_LIMIT),
        name="nsa_compress",
    )(r0, r1, pos, w1, w2)


def _tile_heads(x):
    return jnp.concatenate([x] * NSA_HPG, axis=1)


def _nsa_kernel(q_ref, cmp_ref, slc_ref, win_ref, gate_ref, expand_ref, o_ref,
                ks_ref, vst_ref, kw_ref, vwt_ref, m_ref, l_ref, acc_ref, *, S, kc):
    qb = pl.program_id(2)
    q0 = qb * Q_BLOCK
    n_slc = S // SLC_BLOCK
    n_cmp = S // CMP_STRIDE
    hd, H = NSA_HD, NSA_HPG

    @pl.when(qb == 0)
    def _():
        slab = slc_ref[...]
        ks_ref[...] = slab[:, :hd].astype(BF16)
        for c in range(S // kc):
            vst_ref[c] = slab[c * kc:(c + 1) * kc].T[hd:].astype(BF16)
        slab = win_ref[...]
        kw_ref[...] = slab[:, :hd].astype(BF16)
        for c in range(S // Q_BLOCK):
            vwt_ref[c] = slab[c * Q_BLOCK:(c + 1) * Q_BLOCK].T[hd:].astype(BF16)

    q_t = (q_ref[...] * (hd ** -0.5)).T
    q_f = jnp.concatenate([q_t[h * hd:(h + 1) * hd] for h in range(H)], axis=1)
    q_cat = q_f.astype(BF16)
    q_cat2 = (q_f * LOG2E).astype(BF16)
    t_row = q0 + lax.broadcasted_iota(jnp.int32, (1, Q_BLOCK), 1)

    cmp_t = cmp_ref[0, 0].T
    k_cmp = cmp_ref[0, 0, :, :hd].astype(BF16)
    cmp_end = lax.broadcasted_iota(jnp.int32, (n_cmp, 1), 0) * CMP_STRIDE + (CMP_BLOCK - 1)
    s = _dot(k_cmp, q_cat) + _tile_heads(jnp.where(cmp_end <= t_row, 0.0, NEG))
    e = jnp.exp(s - jnp.max(s, axis=0, keepdims=True))
    any_c = _tile_heads(jnp.where(t_row >= CMP_BLOCK - 1, 1.0, 0.0))
    p_c = e * (any_c / jnp.maximum(jnp.sum(e, axis=0, keepdims=True), 1e-30))
    o_c = _dot(cmp_t[hd:].astype(BF16), p_c.astype(BF16))
    p_sum = p_c[:, :Q_BLOCK]
    for h in range(1, H):
        p_sum = p_sum + p_c[:, h * Q_BLOCK:(h + 1) * Q_BLOCK]

    sj = lax.broadcasted_iota(jnp.int32, (n_slc, n_cmp), 0) * SLC_BLOCK
    cn = lax.broadcasted_iota(jnp.int32, (n_slc, n_cmp), 1) * CMP_STRIDE
    overlap = jnp.where(cn <= sj + (SLC_BLOCK - 1),
                        jnp.where(cn + (CMP_BLOCK - 1) >= sj, 1.0, 0.0), 0.0).astype(BF16)
    imp = _dot_exact_lhs(overlap, p_sum)
    jblk = lax.broadcasted_iota(jnp.int32, (n_slc, Q_BLOCK), 0)
    bt = (q0 + lax.broadcasted_iota(jnp.int32, (n_slc, Q_BLOCK), 1)) // SLC_BLOCK
    back = bt - jblk
    bonus = jnp.where(jblk == 0, FORCE_BONUS,
                      jnp.where(back >= 0, jnp.where(back < FORCED_LOCAL, FORCE_BONUS, 0.0), 0.0))
    score = jnp.where(jblk <= bt, imp + bonus, NEG)
    SUB = 8
    groups = [score[r:r + SUB] for r in range(0, n_slc, SUB)]
    ranks = [jnp.zeros((SUB, Q_BLOCK), F32) for _ in groups]
    for i in range(n_slc):
        row = score[i:i + 1, :]
        for gi, grp in enumerate(groups):
            lo = gi * SUB
            if lo > i:
                ahead = row >= grp
            elif lo + SUB - 1 <= i:
                ahead = row > grp
            else:
                ahead = jnp.where(jblk[lo:lo + SUB] > i, jnp.where(row >= grp, 1.0, 0.0),
                                  jnp.where(row > grp, 1.0, 0.0)) > 0.5
            ranks[gi] = ranks[gi] + jnp.where(ahead, 1.0, 0.0)
    rank = jnp.concatenate(ranks, axis=0)
    sel_bias = jnp.where(rank < float(min(N_SELECT, n_slc)), 0.0, NEG).astype(BF16)

    m_ref[...] = jnp.full_like(m_ref, NEG)
    l_ref[...] = jnp.zeros_like(l_ref)
    acc_ref[...] = jnp.zeros_like(acc_ref)
    k_row = lax.broadcasted_iota(jnp.int32, (kc, 1), 0)

    def chunk(c, carry):
        r0 = pl.multiple_of(c * kc, kc)
        picked = _dot(expand_ref[pl.ds(r0, kc), :], sel_bias)
        bias = _tile_heads(jnp.where(k_row + c * kc <= t_row, picked, NEG))
        s = _dot(ks_ref[pl.ds(r0, kc), :], q_cat2) + bias
        m_old = m_ref[...]
        m_new = jnp.maximum(m_old, jnp.max(s, axis=0, keepdims=True))
        alpha = jnp.exp2(m_old - m_new)
        p = jnp.exp2(s - m_new)
        l_ref[...] = alpha * l_ref[...] + jnp.sum(p, axis=0, keepdims=True)
        acc_ref[...] = alpha * acc_ref[...] + _dot(vst_ref[c], p.astype(BF16))
        m_ref[...] = m_new
        return carry

    lax.fori_loop(0, (q0 + Q_BLOCK - 1) // kc + 1, chunk, 0)
    o_s = acc_ref[...] * (1.0 / jnp.maximum(l_ref[...], 1e-30))

    wlen = WINDOW + Q_BLOCK
    w0 = pl.multiple_of(jnp.maximum(q0 - WINDOW, 0), Q_BLOCK)
    dist = t_row - (w0 + lax.broadcasted_iota(jnp.int32, (wlen, 1), 0))
    bias_w = jnp.where(dist >= 0, jnp.where(dist < WINDOW, 0.0, NEG), NEG)
    s = _dot(kw_ref[pl.ds(w0, wlen), :], q_cat2) + _tile_heads(bias_w)
    e = jnp.exp2(s - jnp.max(s, axis=0, keepdims=True))
    l_w = jnp.sum(e, axis=0, keepdims=True)
    e = e.astype(BF16)
    wb = w0 // Q_BLOCK
    o_w = _dot(vwt_ref[wb], e[:Q_BLOCK])
    for j in range(1, wlen // Q_BLOCK):
        o_w = o_w + _dot(vwt_ref[wb + j], e[j * Q_BLOCK:(j + 1) * Q_BLOCK])
    o_w = o_w * (1.0 / jnp.maximum(l_w, 1e-30))

    sig_t = _sigmoid(gate_ref[...]).T
    gate = lambda c: jnp.concatenate([sig_t[c * H + h:c * H + h + 1] for h in range(H)], axis=1)
    out_t = gate(0) * o_c + gate(1) * o_s + gate(2) * o_w
    out = jnp.concatenate([out_t[:, h * Q_BLOCK:(h + 1) * Q_BLOCK] for h in range(H)], axis=0)
    o_ref[...] = out.T.astype(o_ref.dtype)


def _nsa(main, kvcmp, B, S, kc=1024):
    T = B * S
    nqb = S // Q_BLOCK
    G = NSA_KV_GROUPS
    assert S % kc == 0 and S >= WINDOW + Q_BLOCK
    n_slc = S // SLC_BLOCK
    expand = (jnp.arange(S, dtype=jnp.int32)[:, None] // SLC_BLOCK
              == jnp.arange(n_slc, dtype=jnp.int32)[None, :]).astype(BF16)
    return pl.pallas_call(
        functools.partial(_nsa_kernel, S=S, kc=kc),
        grid=(B, G, nqb),
        in_specs=[
            pl.BlockSpec((Q_BLOCK, NQ_W), lambda b, g, i: (b * nqb + i, COL_NQ // NQ_W + g)),
            pl.BlockSpec((1, 1, S // CMP_STRIDE, LANES), lambda b, g, i: (b, g, 0, 0)),
            pl.BlockSpec((S, LANES), lambda b, g, i: (b, COL_SLC // LANES + g)),
            pl.BlockSpec((S, LANES), lambda b, g, i: (b, COL_WIN // LANES + g)),
            pl.BlockSpec((Q_BLOCK, LANES), lambda b, g, i: (b * nqb + i, COL_GATE // LANES + g)),
            pl.BlockSpec((S, n_slc), lambda b, g, i: (0, 0)),
        ],
        out_specs=pl.BlockSpec((Q_BLOCK, NQ_W), lambda b, g, i: (b * nqb + i, g)),
        out_shape=jax.ShapeDtypeStruct((T, G * NQ_W), BF16),
        scratch_shapes=[
            pltpu.VMEM((S, NSA_HD), BF16),
            pltpu.VMEM((S // kc, NSA_HD, kc), BF16),
            pltpu.VMEM((S, NSA_HD), BF16),
            pltpu.VMEM((S // Q_BLOCK, NSA_HD, Q_BLOCK), BF16),
            pltpu.VMEM((1, NSA_HPG * Q_BLOCK), F32),
            pltpu.VMEM((1, NSA_HPG * Q_BLOCK), F32),
            pltpu.VMEM((NSA_HD, NSA_HPG * Q_BLOCK), F32)],
        compiler_params=pltpu.CompilerParams(
            dimension_semantics=("parallel", "parallel", "arbitrary"), vmem_limit_bytes=VMEM_LIMIT),
        name="nsa_attention",
    )(main, kvcmp, main, main, main, expand)


HGRN_SUB = HGRN_CHUNK // 2
HGRN_SAFE_RANGE = 60.0


def _hgrn_kernel(q_ref, f_ref, i_ref, g_ref, lbl_ref, ng_ref, o_ref,
                 st_ref, q_s, k_s, b_s, v_s, oi_s, *, rb, layer_j):
    C, SB = HGRN_CHUNK, HGRN_SUB
    nch = rb // C

    @pl.when(pl.program_id(2) == 0)
    def _():
        st_ref[...] = jnp.zeros_like(st_ref)

    logits = lbl_ref[...]
    e = jnp.exp(logits - jnp.max(logits, axis=0, keepdims=True))
    sm = e / jnp.sum(e, axis=0, keepdims=True)
    lb = jnp.sum(sm[:layer_j + 1], axis=0, keepdims=True) - sm[0:1]
    lb = jnp.maximum(lb, 0.0)

    fp = f_ref[...]
    qp = q_ref[...]
    log_sig = jnp.minimum(fp, 0.0) - jnp.log1p(jnp.exp(-jnp.abs(fp)))
    c_term = jnp.log1p(-lb) + log_sig
    a_term = jnp.log(lb)
    log_f = jnp.maximum(a_term, c_term) + jnp.log1p(jnp.exp(-jnp.abs(a_term - c_term)))
    k = (1.0 - lb) * _sigmoid(-fp)
    q = qp * _sigmoid(qp) * (HGRN_DK ** -0.5)
    v = i_ref[...]

    ri = lax.broadcasted_iota(jnp.int32, (rb, rb), 0)
    ci = lax.broadcasted_iota(jnp.int32, (rb, rb), 1)
    same_chunk = (ri // C) == (ci // C)
    lower = same_chunk & (ci <= ri)
    b = _dot_exact_lhs(jnp.where(lower, 1.0, 0.0).astype(BF16), log_f)

    span = jnp.zeros((1, HGRN_DK), F32)
    for s0 in range(0, rb, SB):
        span = jnp.maximum(span, b[s0:s0 + 1] - b[s0 + SB - 1:s0 + SB])
    safe = jnp.max(span) <= HGRN_SAFE_RANGE

    q_s[...] = q
    k_s[...] = k
    b_s[...] = b
    v_s[...] = v

    @pl.when(safe)
    def _():
        qd, kd, qo, ko = [], [], [], []
        for s0 in range(0, rb, SB):
            bb = b[s0:s0 + SB]
            ref = b[s0 + SB // 2:s0 + SB // 2 + 1]
            qd.append(q[s0:s0 + SB] * jnp.exp(bb - ref))
            kd.append(k[s0:s0 + SB] * jnp.exp(ref - bb))
            if (s0 // SB) % 2 == 0:
                end = b[s0 + SB - 1:s0 + SB]
                ko.append(k[s0:s0 + SB] * jnp.exp(end - bb))
                qo.append(jnp.zeros((SB, HGRN_DK), F32))
            else:
                end = b[s0 - 1:s0]
                qo.append(q[s0:s0 + SB] * jnp.exp(bb - end))
                ko.append(jnp.zeros((SB, HGRN_DK), F32))
        cat = lambda xs: jnp.concatenate(xs, axis=0).astype(BF16)
        a_diag = _dot_nt(cat(qd), cat(kd))
        a_off = _dot_nt(cat(qo), cat(ko))
        same_sub = (ri // SB) == (ci // SB)
        attn = jnp.where(lower, jnp.where(same_sub, a_diag, a_off), 0.0)
        oi_s[...] = _dot(attn.astype(BF16), v.astype(BF16))

    @pl.when(jnp.logical_not(safe))
    def _():
        def row(n, carry):
            c0 = pl.multiple_of((n // C) * C, C)
            qn = q_s[pl.ds(n, 1), :]
            bn = b_s[pl.ds(n, 1), :]
            kb = k_s[pl.ds(c0, C), :]
            bb = b_s[pl.ds(c0, C), :]
            vb = v_s[pl.ds(c0, C), :]
            rows = c0 + lax.broadcasted_iota(jnp.int32, (C, 1), 0)
            decay = jnp.exp(jnp.where(rows <= n, bn - bb, -jnp.inf))
            a = jnp.sum(qn * decay * kb, axis=-1, keepdims=True)
            oi_s[pl.ds(n, 1), :] = jnp.sum(a * vb, axis=0, keepdims=True)
            return carry
        lax.fori_loop(0, rb, row, 0)

    qb = (q * jnp.exp(b)).astype(BF16)
    vb16 = v.astype(BF16)
    st = st_ref[...]
    outs = []
    for c in range(nch):
        sl = slice(c * C, (c + 1) * C)
        b_last = b[(c + 1) * C - 1:(c + 1) * C]
        outs.append(oi_s[sl, :] + _dot_nt(qb[sl], st.astype(BF16)))
        k_dec = (k[sl] * jnp.exp(b_last - b[sl])).astype(BF16)
        st = jnp.exp(b_last) * st + _dot_tn(vb16[sl], k_dec)
    st_ref[...] = st
    o = jnp.concatenate(outs, axis=0)
    gp = g_ref[...]
    o_ref[...] = (_rms(o, ng_ref[...]) * _sigmoid(gp)).astype(o_ref.dtype)


def _hgrn(proj, lb_logits, norm_g, layer_j, B, S, rb=256):
    T = B * S
    nrb = S // rb
    H = HGRN_HEADS
    n_layers = lb_logits.shape[0]
    blk = lambda c: pl.BlockSpec((rb, HGRN_DK), lambda b, h, r, c=c: (b * nrb + r, c * H + h))
    vm = lambda: pltpu.VMEM((rb, HGRN_DK), F32)
    return pl.pallas_call(
        functools.partial(_hgrn_kernel, rb=rb, layer_j=layer_j),
        grid=(B, H, nrb),
        in_specs=[blk(0), blk(1), blk(2), blk(3),
                  pl.BlockSpec((n_layers, HGRN_DK), lambda b, h, r: (0, h)),
                  pl.BlockSpec((1, HGRN_DV), lambda b, h, r: (0, 0))],
        out_specs=pl.BlockSpec((rb, HGRN_DV), lambda b, h, r: (b * nrb + r, h)),
        out_shape=jax.ShapeDtypeStruct((T, H * HGRN_DV), BF16),
        scratch_shapes=[pltpu.VMEM((HGRN_DV, HGRN_DK), F32), vm(), vm(), vm(), vm(), vm()],
        compiler_params=pltpu.CompilerParams(
            dimension_semantics=("parallel", "parallel", "arbitrary"), vmem_limit_bytes=VMEM_LIMIT),
        name="hgrn2",
    )(proj, proj, proj, proj, lb_logits, norm_g.reshape(1, HGRN_DV))


def _mix_ffn_kernel(*refs, n_mix):
    x_ref = refs[0]
    mix_refs = refs[1:1 + n_mix]
    wo_refs = refs[1 + n_mix:1 + 2 * n_mix]
    g_ref, w1_ref, w3_ref, w2_ref, o_ref, x1_s, h_s, acc_s = refs[1 + 2 * n_mix:]
    f = pl.program_id(1)

    @pl.when(f == 0)
    def _():
        x1 = x_ref[...]
        for m_ref, wo_ref in zip(mix_refs, wo_refs):
            x1 = x1 + _dot(m_ref[...], wo_ref[...])
        x1_s[...] = x1
        h_s[...] = _rms(x1, g_ref[...]).astype(BF16)

    h = h_s[...]
    a = _dot(h, w1_ref[...])
    u = (a * _sigmoid(a) * _dot(h, w3_ref[...])).astype(BF16)
    part = _dot(u, w2_ref[...])

    @pl.when(f == 0)
    def _():
        acc_s[...] = part

    @pl.when(f > 0)
    def _():
        acc_s[...] += part

    @pl.when(f == pl.num_programs(1) - 1)
    def _():
        o_ref[...] = x1_s[...] + acc_s[...]


def _mix_ffn(x2, mixes, wos, g, w1, w3, w2, tm=1024, tf=256):
    T, D = x2.shape
    F = w1.shape[1]
    assert T % tm == 0 and F % tf == 0
    n_mix = len(mixes)
    in_specs = [pl.BlockSpec((tm, D), lambda i, f: (i, 0))]
    in_specs += [pl.BlockSpec((tm, m.shape[1]), lambda i, f: (i, 0)) for m in mixes]
    in_specs += [pl.BlockSpec(w.shape, lambda i, f: (0, 0)) for w in wos]
    in_specs += [pl.BlockSpec((1, D), lambda i, f: (0, 0)),
                 pl.BlockSpec((D, tf), lambda i, f: (0, f)),
                 pl.BlockSpec((D, tf), lambda i, f: (0, f)),
                 pl.BlockSpec((tf, D), lambda i, f: (f, 0))]
    return pl.pallas_call(
        functools.partial(_mix_ffn_kernel, n_mix=n_mix),
        grid=(T // tm, F // tf),
        in_specs=in_specs,
        out_specs=pl.BlockSpec((tm, D), lambda i, f: (i, 0)),
        out_shape=jax.ShapeDtypeStruct((T, D), F32),
        scratch_shapes=[pltpu.VMEM((tm, D), F32), pltpu.VMEM((tm, D), BF16), pltpu.VMEM((tm, D), F32)],
        compiler_params=pltpu.CompilerParams(
            dimension_semantics=("parallel", "arbitrary"), vmem_limit_bytes=VMEM_LIMIT),
        name="mix_ffn",
    )(x2, *mixes, *wos, g.reshape(1, D), w1, w3, w2)


def _final_norm_kernel(x_ref, g_ref, o_ref):
    o_ref[...] = _rms(x_ref[...], g_ref[...])


def _final_norm(x2, g, tm=1024):
    T, D = x2.shape
    return pl.pallas_call(
        _final_norm_kernel,
        grid=(T // tm,),
        in_specs=[pl.BlockSpec((tm, D), lambda i: (i, 0)), pl.BlockSpec((1, D), lambda i: (0, 0))],
        out_specs=pl.BlockSpec((tm, D), lambda i: (i, 0)),
        out_shape=jax.ShapeDtypeStruct((T, D), F32),
        compiler_params=pltpu.CompilerParams(
            dimension_semantics=("parallel",), vmem_limit_bytes=VMEM_LIMIT),
        name="final_norm",
    )(x2, g.reshape(1, D))


def _even_w_in_layout(w):
    D = w.shape[0]
    hd, G = NSA_HD, NSA_KV_GROUPS
    o_kc = 4 * RET_W + NSA_HEADS * hd
    kv = lambda idx, g: w[:, o_kc + idx * G * hd + g * hd:o_kc + idx * G * hd + (g + 1) * hd]
    o_ng = o_kc + 6 * G * hd
    cols = [w[:, :o_kc]]
    for idx_k, idx_v in ((2, 3), (4, 5)):
        for g in range(G):
            cols += [kv(idx_k, g), kv(idx_v, g)]
    for g in range(G):
        gate = [w[:, o_ng + c * NSA_HEADS + g * NSA_HPG:o_ng + c * NSA_HEADS + (g + 1) * NSA_HPG]
                for c in range(3)]
        cols += gate + [jnp.zeros((D, LANES - 3 * NSA_HPG), w.dtype)]
    for g in range(G):
        cols += [kv(0, g), kv(1, g)]
    out = jnp.concatenate(cols, axis=1)
    assert out.shape[1] == EVEN_N
    return out.astype(BF16)


def _compress_weights(pos_k, w1_k, w2_k, pos_v, w1_v, w2_v):
    hd = NSA_HD
    z = jnp.zeros((CMP_BLOCK, hd, hd), F32)
    w1k = w1_k.reshape(CMP_BLOCK, hd, hd)
    w1v = w1_v.reshape(CMP_BLOCK, hd, hd)
    top = jnp.concatenate([w1k, z], axis=2)
    bot = jnp.concatenate([z, w1v], axis=2)
    w1 = jnp.concatenate([top, bot], axis=1)
    w1 = w1.reshape(2, CMP_STRIDE * LANES, LANES).astype(BF16)
    pos = jnp.concatenate([pos_k, pos_v], axis=1).reshape(2, 1, CMP_STRIDE * LANES)
    zz = jnp.zeros((hd, hd), F32)
    w2 = jnp.concatenate([jnp.concatenate([w2_k, zz], axis=1),
                          jnp.concatenate([zz, w2_v], axis=1)], axis=0).astype(BF16)
    return pos, w1, w2


def _rotary_tables(S):
    half = RET_DK // 2
    inv = ROPE_BASE ** (-jnp.arange(half, dtype=F32) / half)
    ang = jnp.arange(S, dtype=F32)[:, None] * inv[None, :]
    cos, sin = jnp.cos(ang), jnp.sin(ang)
    return jnp.concatenate([cos, cos], axis=1), jnp.concatenate([-sin, sin], axis=1)


def kernel(x, norm_mix_g, norm_ffn_g, final_norm_g, even_w_in, even_w_out, cmp_pos_k, cmp_w1_k, cmp_w2_k, cmp_pos_v, cmp_w1_v, cmp_w2_v, odd_w_in, odd_w_out, hgrn_norm_g, hgrn_lb_logits, ffn_w1, ffn_w3, ffn_w2):
    B, S, D = x.shape
    x2 = x.reshape(B * S, D)
    cosf, sinf = _rotary_tables(S)
    for layer in range(DEPTH):
        j = layer // 2
        if layer % 2 == 0:
            w_in = _even_w_in_layout(even_w_in[j])
            main, cmp0, cmp1 = _norm_proj(x2, norm_mix_g[layer], w_in, (EVEN_MAIN, LANES, LANES))
            pos, cw1, cw2 = _compress_weights(cmp_pos_k[j], cmp_w1_k[j], cmp_w2_k[j],
                                              cmp_pos_v[j], cmp_w1_v[j], cmp_w2_v[j])
            kvcmp = _compress(cmp0, cmp1, pos, cw1, cw2, B, S)
            o_r = _retention(main, cosf, sinf, B, S)
            o_n = _nsa(main, kvcmp, B, S)
            w_out = even_w_out[j].astype(BF16)
            mixes, wos = (o_r, o_n), (w_out[:RET_W], w_out[RET_W:])
        else:
            (proj,) = _norm_proj(x2, norm_mix_g[layer], odd_w_in[j].astype(BF16), (4 * D,))
            o_h = _hgrn(proj, hgrn_lb_logits, hgrn_norm_g[j], j, B, S)
            mixes, wos = (o_h,), (odd_w_out[j].astype(BF16),)
        x2 = _mix_ffn(x2, mixes, wos, norm_ffn_g[layer],
                      ffn_w1[layer].astype(BF16), ffn_w3[layer].astype(BF16), ffn_w2[layer].astype(BF16))
    return _final_norm(x2, final_norm_g).reshape(B, S, D)
```

```python
import functools
import math

import jax
import jax.numpy as jnp
from jax import lax
from jax.experimental import pallas as pl
from jax.experimental.pallas import tpu as pltpu

F32 = jnp.float32
BF16 = jnp.bfloat16

D_MODEL = 1024
DEPTH = 4
RET_HEADS = 4
RET_DK = 128
RET_DV = 128
ROPE_BASE = 10000.0
NSA_HEADS = 8
NSA_KV_GROUPS = 2
NSA_HPG = NSA_HEADS // NSA_KV_GROUPS
NSA_HD = 64
CMP_BLOCK = 32
CMP_STRIDE = 16
SLC_BLOCK = 64
N_SELECT = 8
FORCED_LOCAL = 2
FORCE_BONUS = 1.0e4
WINDOW = 512
Q_BLOCK = 128
HGRN_HEADS = 8
HGRN_DK = D_MODEL // HGRN_HEADS
HGRN_DV = D_MODEL // HGRN_HEADS
HGRN_CHUNK = 64
D_FF = -(-8 * D_MODEL // (3 * 256)) * 256
EPS = 1e-6
EVEN_MIX = RET_HEADS * RET_DV + NSA_HEADS * NSA_HD

LANES = 128
VMEM_LIMIT = 56 * 1024 * 1024
NEG = -1e30
M_FLOOR = -1e29
LOG2E = math.log2(math.e)

RET_W = RET_HEADS * RET_DK
COL_RQ, COL_RK, COL_RV, COL_RG = 0, RET_W, 2 * RET_W, 3 * RET_W
COL_NQ = 4 * RET_W
NQ_W = NSA_HPG * NSA_HD
COL_SLC = COL_NQ + NSA_KV_GROUPS * NQ_W
COL_WIN = COL_SLC + NSA_KV_GROUPS * LANES
COL_GATE = COL_WIN + NSA_KV_GROUPS * LANES
EVEN_MAIN = COL_GATE + NSA_KV_GROUPS * LANES
EVEN_N = EVEN_MAIN + NSA_KV_GROUPS * LANES


def _dot(a, b):
    return jnp.dot(a, b, preferred_element_type=F32)


def _dot_nt(a, b):
    return lax.dot_general(a, b, (((1,), (1,)), ((), ())), preferred_element_type=F32)


def _dot_tn(a, b):
    return lax.dot_general(a, b, (((0,), (0,)), ((), ())), preferred_element_type=F32)


def _split3(x):
    hi = x.astype(BF16)
    r = x - hi.astype(F32)
    mid = r.astype(BF16)
    lo = (r - mid.astype(F32)).astype(BF16)
    return hi, mid, lo


def _dot_exact_lhs(sel, x):
    hi, mid, lo = _split3(x)
    return _dot(sel, hi) + _dot(sel, mid) + _dot(sel, lo)


def _dot_exact_rhs(x, sel):
    hi, mid, lo = _split3(x)
    return _dot(hi, sel) + _dot(mid, sel) + _dot(lo, sel)


def _rms(x, g):
    return x * lax.rsqrt(jnp.mean(x * x, axis=-1, keepdims=True) + EPS) * g


def _sigmoid(x):
    return 1.0 / (1.0 + jnp.exp(-x))


def _norm_proj_kernel(x_ref, g_ref, w_ref, *o_refs):
    h = _rms(x_ref[...], g_ref[...]).astype(BF16)
    off = 0
    for o_ref in o_refs:
        wd = o_ref.shape[1]
        o_ref[...] = _dot(h, w_ref[:, off:off + wd])
        off += wd


def _norm_proj(x2, g, w, widths, tm=512):
    T, D = x2.shape
    N = w.shape[1]
    assert sum(widths) == N and T % tm == 0
    return pl.pallas_call(
        _norm_proj_kernel,
        grid=(T // tm,),
        in_specs=[
            pl.BlockSpec((tm, D), lambda i: (i, 0)),
            pl.BlockSpec((1, D), lambda i: (0, 0)),
            pl.BlockSpec((D, N), lambda i: (0, 0)),
        ],
        out_specs=[pl.BlockSpec((tm, wd), lambda i: (i, 0)) for wd in widths],
        out_shape=[jax.ShapeDtypeStruct((T, wd), F32) for wd in widths],
        compiler_params=pltpu.CompilerParams(
            dimension_semantics=("parallel",), vmem_limit_bytes=VMEM_LIMIT),
        name="norm_proj",
    )(x2, g.reshape(1, D), w)


def _retention_kernel(q_ref, k_ref, v_ref, g_ref, cos_ref, sin_ref, o_ref, state_ref, *, cb):
    @pl.when(pl.program_id(1) == 0)
    def _():
        state_ref[...] = jnp.zeros_like(state_ref)

    cosf = cos_ref[...]
    sinf = sin_ref[...]
    ri = lax.broadcasted_iota(jnp.int32, (cb, cb), 0)
    ci = lax.broadcasted_iota(jnp.int32, (cb, cb), 1)
    diff = (ri - ci).astype(F32)
    causal = ri >= ci
    idx = lax.broadcasted_iota(jnp.int32, (cb, 1), 0).astype(F32)
    for h in range(RET_HEADS):
        lg = math.log(1.0 - 2.0 ** (-5.0 - h))
        sl = slice(h * RET_DK, (h + 1) * RET_DK)
        q = q_ref[:, sl]
        k = k_ref[:, sl]
        v = v_ref[:, sl].astype(BF16)
        qr = q * cosf + pltpu.roll(q, RET_DK // 2, 1) * sinf
        kr = (k * cosf + pltpu.roll(k, RET_DK // 2, 1) * sinf) * (RET_DK ** -0.5)
        dmat = jnp.where(causal, jnp.exp(lg * jnp.where(causal, diff, 0.0)), 0.0)
        scores = _dot_nt(qr.astype(BF16), kr.astype(BF16)) * dmat
        o = _dot(scores.astype(BF16), v)
        q_dec = qr * jnp.exp(lg * (idx + 1.0))
        state = state_ref[h]
        o = o + _dot(q_dec.astype(BF16), state.astype(BF16))
        k_dec = kr * jnp.exp(lg * (cb - 1.0 - idx))
        state_ref[h] = math.exp(lg * cb) * state + _dot_tn(k_dec.astype(BF16), v)
        mu = jnp.mean(o, axis=-1, keepdims=True)
        oc = o - mu
        var = jnp.mean(oc * oc, axis=-1, keepdims=True)
        gate = g_ref[:, sl]
        o_ref[:, sl] = (oc * lax.rsqrt(var + 1e-5) * (gate * _sigmoid(gate))).astype(o_ref.dtype)


def _retention(main, cosf, sinf, B, S, cb=256):
    T = B * S
    nc = S // cb
    blk = lambda c: pl.BlockSpec((cb, RET_W), lambda b, i, c=c: (b * nc + i, c))
    tab = pl.BlockSpec((cb, RET_DK), lambda b, i: (i, 0))
    return pl.pallas_call(
        functools.partial(_retention_kernel, cb=cb),
        grid=(B, nc),
        in_specs=[blk(COL_RQ // RET_W), blk(COL_RK // RET_W), blk(COL_RV // RET_W),
                  blk(COL_RG // RET_W), tab, tab],
        out_specs=pl.BlockSpec((cb, RET_W), lambda b, i: (b * nc + i, 0)),
        out_shape=jax.ShapeDtypeStruct((T, RET_W), BF16),
        scratch_shapes=[pltpu.VMEM((RET_HEADS, RET_DK, RET_DV), F32)],
        compiler_params=pltpu.CompilerParams(
            dimension_semantics=("parallel", "arbitrary"), vmem_limit_bytes=VMEM_LIMIT),
        name="retention",
    )(main, main, main, main, cosf, sinf)


def _gelu_tanh(y):
    return 0.5 * y * (1.0 + jnp.tanh(math.sqrt(2.0 / math.pi) * (y + 0.044715 * (y * y * y))))


def _compress_kernel(r0_ref, r1_ref, pos_ref, w1_ref, w2_ref, o_ref):
    for g, r_ref in enumerate((r0_ref, r1_ref)):
        r = r_ref[0]
        n = r.shape[0]
        y_lo = _dot((r + pos_ref[0]).astype(BF16), w1_ref[0])
        y_hi = _dot((r + pos_ref[1]).astype(BF16), w1_ref[1])
        y = y_lo + pltpu.roll(y_hi, n - 1, 0)
        out = _dot(_gelu_tanh(y).astype(BF16), w2_ref[...])
        row = lax.broadcasted_iota(jnp.int32, out.shape, 0)
        o_ref[0, g] = jnp.where(row < n - 1, out, 0.0)


def _compress(cmp0, cmp1, pos, w1, w2, B, S):
    nr = S // CMP_STRIDE
    kw = CMP_STRIDE * LANES
    r0 = cmp0.reshape(B, nr, kw)
    r1 = cmp1.reshape(B, nr, kw)
    rspec = pl.BlockSpec((1, nr, kw), lambda b: (b, 0, 0))
    return pl.pallas_call(
        _compress_kernel,
        grid=(B,),
        in_specs=[rspec, rspec,
                  pl.BlockSpec((2, 1, kw), lambda b: (0, 0, 0)),
                  pl.BlockSpec((2, kw, LANES), lambda b: (0, 0, 0)),
                  pl.BlockSpec((LANES, LANES), lambda b: (0, 0))],
        out_specs=pl.BlockSpec((1, NSA_KV_GROUPS, nr, LANES), lambda b: (b, 0, 0, 0)),
        out_shape=jax.ShapeDtypeStruct((B, NSA_KV_GROUPS, nr, LANES), F32),
        compiler_params=pltpu.CompilerParams(
            dimension_semantics=("parallel",), vmem_limit_bytes=VMEM_LIMIT),
        name="nsa_compress",
    )(r0, r1, pos, w1, w2)


NSA_WBLK = LANES


def _tile_heads(x):
    return jnp.concatenate([x] * NSA_HPG, axis=1)


def _nsa_kernel(q_ref, cmp_ref, slc_ref, win_ref, gate_ref, expand_ref, o_ref,
                ks_ref, vst_ref, kw_ref, vwt_ref, m_ref, l_ref, acc_ref, *, S, kc):
    QT = q_ref.shape[0]
    qb = pl.program_id(2)
    q0 = qb * QT
    n_slc = S // SLC_BLOCK
    n_cmp = S // CMP_STRIDE
    hd, H = NSA_HD, NSA_HPG

    @pl.when(qb == 0)
    def _():
        slab = slc_ref[...]
        ks_ref[:, :hd] = slab[:, :hd].astype(BF16)
        ks_ref[:, hd:] = expand_ref[...]
        for c in range(S // kc):
            vst_ref[c] = slab[c * kc:(c + 1) * kc].T[hd:].astype(BF16)
        slab = win_ref[...]
        kw_ref[...] = slab[:, :hd].astype(BF16)
        for c in range(S // NSA_WBLK):
            vwt_ref[c] = slab[c * NSA_WBLK:(c + 1) * NSA_WBLK].T[hd:].astype(BF16)

    q_t = (q_ref[...] * (hd ** -0.5)).T
    q_f = jnp.concatenate([q_t[h * hd:(h + 1) * hd] for h in range(H)], axis=1)
    q_cat = q_f.astype(BF16)
    q_cat2 = (q_f * LOG2E).astype(BF16)
    t_row = q0 + lax.broadcasted_iota(jnp.int32, (1, QT), 1)

    wlen = WINDOW + QT
    w0 = pl.multiple_of(jnp.maximum(q0 - WINDOW, 0), NSA_WBLK)
    dist = t_row - (w0 + lax.broadcasted_iota(jnp.int32, (wlen, 1), 0))
    bias_w = jnp.where(dist >= 0, jnp.where(dist < WINDOW, 0.0, NEG), NEG)
    s = _dot(kw_ref[pl.ds(w0, wlen), :], q_cat2) + _tile_heads(bias_w)
    e = jnp.exp2(s - jnp.max(s, axis=0, keepdims=True))
    l_w = jnp.sum(e, axis=0, keepdims=True)
    e = e.astype(BF16)
    wb = w0 // NSA_WBLK
    o_w = _dot(vwt_ref[wb], e[:NSA_WBLK])
    for j in range(1, wlen // NSA_WBLK):
        o_w = o_w + _dot(vwt_ref[wb + j], e[j * NSA_WBLK:(j + 1) * NSA_WBLK])
    o_w = o_w * (1.0 / jnp.maximum(l_w, 1e-30))

    cmp_t = cmp_ref[0, 0].T
    k_cmp = cmp_ref[0, 0, :, :hd].astype(BF16)
    cmp_end = lax.broadcasted_iota(jnp.int32, (n_cmp, 1), 0) * CMP_STRIDE + (CMP_BLOCK - 1)
    s = _dot(k_cmp, q_cat) + _tile_heads(jnp.where(cmp_end <= t_row, 0.0, NEG))
    e = jnp.exp(s - jnp.max(s, axis=0, keepdims=True))
    any_c = _tile_heads(jnp.where(t_row >= CMP_BLOCK - 1, 1.0, 0.0))
    p_c = e * (any_c / jnp.maximum(jnp.sum(e, axis=0, keepdims=True), 1e-30))
    o_c = _dot(cmp_t[hd:].astype(BF16), p_c.astype(BF16))
    p_sum = p_c[:, :QT]
    for h in range(1, H):
        p_sum = p_sum + p_c[:, h * QT:(h + 1) * QT]

    sj = lax.broadcasted_iota(jnp.int32, (n_slc, n_cmp), 0) * SLC_BLOCK
    cn = lax.broadcasted_iota(jnp.int32, (n_slc, n_cmp), 1) * CMP_STRIDE
    overlap = jnp.where(cn <= sj + (SLC_BLOCK - 1),
                        jnp.where(cn + (CMP_BLOCK - 1) >= sj, 1.0, 0.0), 0.0).astype(BF16)
    imp = _dot_exact_lhs(overlap, p_sum)
    jblk = lax.broadcasted_iota(jnp.int32, (n_slc, QT), 0)
    bt = (q0 + lax.broadcasted_iota(jnp.int32, (n_slc, QT), 1)) // SLC_BLOCK
    back = bt - jblk
    bonus = jnp.where(jblk == 0, FORCE_BONUS,
                      jnp.where(back >= 0, jnp.where(back < FORCED_LOCAL, FORCE_BONUS, 0.0), 0.0))
    score = jnp.where(jblk <= bt, imp + bonus, NEG)
    SUB = 8
    groups = [score[r:r + SUB] for r in range(0, n_slc, SUB)]
    ranks = [jnp.zeros((SUB, QT), F32) for _ in groups]
    for i in range(n_slc):
        row = score[i:i + 1, :]
        for gi, grp in enumerate(groups):
            lo = gi * SUB
            if lo > i:
                ahead = row >= grp
            elif lo + SUB - 1 <= i:
                ahead = row > grp
            else:
                below = lo + lax.broadcasted_iota(jnp.int32, (SUB, QT), 0) > i
                ahead = jnp.where(below, jnp.where(row >= grp, 1.0, 0.0),
                                  jnp.where(row > grp, 1.0, 0.0)) > 0.5
            ranks[gi] = ranks[gi] + jnp.where(ahead, 1.0, 0.0)
    rank = jnp.concatenate(ranks, axis=0)
    sel_bias = jnp.where(rank < float(min(N_SELECT, n_slc)), 0.0, NEG).astype(BF16)

    ns = m_ref.shape[0]
    m_ref[...] = jnp.full_like(m_ref, M_FLOOR)
    l_ref[...] = jnp.zeros_like(l_ref)
    acc_ref[...] = jnp.zeros_like(acc_ref)
    q_aug = jnp.concatenate([q_cat2, _tile_heads(sel_bias)], axis=0)
    gap = (q0 + lax.broadcasted_iota(jnp.int32, (kc, QT), 1)) - lax.broadcasted_iota(jnp.int32, (kc, QT), 0)

    def chunks(it, causal_mask):
        scores = []
        for j in range(ns):
            c = it * ns + j
            s = _dot(ks_ref[pl.ds(pl.multiple_of(c * kc, kc), kc), :], q_aug)
            if causal_mask:
                s = s + _tile_heads(jnp.where(gap >= c * kc, 0.0, NEG))
            scores.append(s)
        for j in range(ns):
            c = it * ns + j
            s = scores[j]
            m_old = m_ref[j]
            m_new = jnp.maximum(m_old, jnp.max(s, axis=0, keepdims=True))
            alpha = jnp.exp2(m_old - m_new)
            p = jnp.exp2(s - m_new)
            l_ref[j] = alpha * l_ref[j] + jnp.sum(p, axis=0, keepdims=True)
            acc_ref[j] = alpha * acc_ref[j] + _dot(vst_ref[c], p.astype(BF16))
            m_ref[j] = m_new

    n_full = q0 // (ns * kc)
    lax.fori_loop(0, n_full, lambda it, carry: (chunks(it, False), carry)[1], 0)
    chunks(n_full, True)
    m_all = m_ref[0]
    for j in range(1, ns):
        m_all = jnp.maximum(m_all, m_ref[j])
    l_all = jnp.zeros_like(m_all)
    acc = jnp.zeros(acc_ref.shape[1:], F32)
    for j in range(ns):
        w = jnp.exp2(m_ref[j] - m_all)
        l_all = l_all + w * l_ref[j]
        acc = acc + w * acc_ref[j]
    o_s = acc * (1.0 / jnp.maximum(l_all, 1e-30))

    sig_t = _sigmoid(gate_ref[...]).T
    gate = lambda c: jnp.concatenate([sig_t[c * H + h:c * H + h + 1] for h in range(H)], axis=1)
    out_t = gate(0) * o_c + gate(1) * o_s + gate(2) * o_w
    out = jnp.concatenate([out_t[:, h * QT:(h + 1) * QT] for h in range(H)], axis=0)
    o_ref[...] = out.T.astype(o_ref.dtype)


def _nsa(main, kvcmp, B, S, kc=128, qt=256, ns=4):
    T = B * S
    nqb = S // qt
    G = NSA_KV_GROUPS
    assert S % (ns * kc) == 0 and (ns * kc) % qt == 0 and qt % NSA_WBLK == 0 and S >= WINDOW + qt
    n_slc = S // SLC_BLOCK
    expand = (jnp.arange(S, dtype=jnp.int32)[:, None] // SLC_BLOCK
              == jnp.arange(n_slc, dtype=jnp.int32)[None, :]).astype(BF16)
    return pl.pallas_call(
        functools.partial(_nsa_kernel, S=S, kc=kc),
        grid=(B, G, nqb),
        in_specs=[
            pl.BlockSpec((qt, NQ_W), lambda b, g, i: (b * nqb + i, COL_NQ // NQ_W + g)),
            pl.BlockSpec((1, 1, S // CMP_STRIDE, LANES), lambda b, g, i: (b, g, 0, 0)),
            pl.BlockSpec((S, LANES), lambda b, g, i: (b, COL_SLC // LANES + g)),
            pl.BlockSpec((S, LANES), lambda b, g, i: (b, COL_WIN // LANES + g)),
            pl.BlockSpec((qt, LANES), lambda b, g, i: (b * nqb + i, COL_GATE // LANES + g)),
            pl.BlockSpec((S, n_slc), lambda b, g, i: (0, 0)),
        ],
        out_specs=pl.BlockSpec((qt, NQ_W), lambda b, g, i: (b * nqb + i, g)),
        out_shape=jax.ShapeDtypeStruct((T, G * NQ_W), BF16),
        scratch_shapes=[
            pltpu.VMEM((S, NSA_HD + n_slc), BF16),
            pltpu.VMEM((S // kc, NSA_HD, kc), BF16),
            pltpu.VMEM((S, NSA_HD), BF16),
            pltpu.VMEM((S // NSA_WBLK, NSA_HD, NSA_WBLK), BF16),
            pltpu.VMEM((ns, 1, NSA_HPG * qt), F32),
            pltpu.VMEM((ns, 1, NSA_HPG * qt), F32),
            pltpu.VMEM((ns, NSA_HD, NSA_HPG * qt), F32)],
        compiler_params=pltpu.CompilerParams(
            dimension_semantics=("parallel", "parallel", "arbitrary"), vmem_limit_bytes=VMEM_LIMIT),
        name="nsa_attention",
    )(main, kvcmp, main, main, main, expand)


HGRN_SUB = HGRN_CHUNK // 2
HGRN_SAFE_RANGE = 60.0


def _hgrn_kernel(q_ref, f_ref, i_ref, g_ref, lbl_ref, ng_ref, o_ref,
                 st_ref, oi_s, oc_s, q_s, k_s, b_s, v_s, *, rb, hp, layer_j):
    C, SB, DK = HGRN_CHUNK, HGRN_SUB, HGRN_DK
    nch = rb // C

    @pl.when(pl.program_id(2) == 0)
    def _():
        st_ref[...] = jnp.zeros_like(st_ref)

    logits = lbl_ref[...]
    e = jnp.exp(logits - jnp.max(logits, axis=0, keepdims=True))
    sm = e / jnp.sum(e, axis=0, keepdims=True)
    lb_all = jnp.maximum(jnp.sum(sm[:layer_j + 1], axis=0, keepdims=True) - sm[0:1], 0.0)

    ri = lax.broadcasted_iota(jnp.int32, (rb, rb), 0)
    ci = lax.broadcasted_iota(jnp.int32, (rb, rb), 1)
    lower = ((ri // C) == (ci // C)) & (ci <= ri)
    same_sub = (ri // SB) == (ci // SB)
    ltri = jnp.where(lower, 1.0, 0.0).astype(BF16)
    cat = lambda xs: jnp.concatenate(xs, axis=0).astype(BF16)

    heads = []
    span = jnp.zeros((1, DK), F32)
    for h in range(hp):
        hs = slice(h * DK, (h + 1) * DK)
        lb = lb_all[:, hs]
        fp = f_ref[:, hs]
        qp = q_ref[:, hs]
        v = i_ref[:, hs]
        log_sig = jnp.minimum(fp, 0.0) - jnp.log(1.0 + jnp.exp(-jnp.abs(fp)))
        c_term = jnp.log1p(-lb) + log_sig
        a_term = jnp.log(lb)
        log_f = jnp.maximum(a_term, c_term) + jnp.log(1.0 + jnp.exp(-jnp.abs(a_term - c_term)))
        k = (1.0 - lb) * _sigmoid(-fp)
        q = qp * _sigmoid(qp) * (DK ** -0.5)
        b = _dot_exact_lhs(ltri, log_f)
        for s0 in range(0, rb, SB):
            span = jnp.maximum(span, b[s0:s0 + 1] - b[s0 + SB - 1:s0 + SB])

        qd, kd, qo, ko = [], [], [], []
        for s0 in range(0, rb, SB):
            bb = b[s0:s0 + SB]
            ref = b[s0 + SB // 2:s0 + SB // 2 + 1]
            qd.append(q[s0:s0 + SB] * jnp.exp(bb - ref))
            kd.append(k[s0:s0 + SB] * jnp.exp(ref - bb))
            if (s0 // SB) % 2 == 0:
                end = b[s0 + SB - 1:s0 + SB]
                ko.append(k[s0:s0 + SB] * jnp.exp(end - bb))
                qo.append(jnp.zeros((SB, DK), F32))
            else:
                end = b[s0 - 1:s0]
                qo.append(q[s0:s0 + SB] * jnp.exp(bb - end))
                ko.append(jnp.zeros((SB, DK), F32))
        a_diag = _dot_nt(cat(qd), cat(kd))
        a_off = _dot_nt(cat(qo), cat(ko))
        attn = jnp.where(lower, jnp.where(same_sub, a_diag, a_off), 0.0)
        vb16 = v.astype(BF16)
        oi_s[h] = _dot(attn.astype(BF16), vb16)

        qb = (q * jnp.exp(b)).astype(BF16)
        st = st_ref[h]
        outs = []
        for c in range(nch):
            sl = slice(c * C, (c + 1) * C)
            b_last = b[(c + 1) * C - 1:(c + 1) * C]
            outs.append(_dot_nt(qb[sl], st.astype(BF16)))
            k_dec = (k[sl] * jnp.exp(b_last - b[sl])).astype(BF16)
            st = jnp.exp(b_last) * st + _dot_tn(vb16[sl], k_dec)
        st_ref[h] = st
        oc_s[h] = jnp.concatenate(outs, axis=0)
        heads.append((q, k, b, v))

    @pl.when(jnp.max(span) > HGRN_SAFE_RANGE)
    def _():
        for h, (q, k, b, v) in enumerate(heads):
            q_s[...] = q
            k_s[...] = k
            b_s[...] = b
            v_s[...] = v

            def row(n, carry):
                c0 = pl.multiple_of((n // C) * C, C)
                qn = q_s[pl.ds(n, 1), :]
                bn = b_s[pl.ds(n, 1), :]
                kb = k_s[pl.ds(c0, C), :]
                bb = b_s[pl.ds(c0, C), :]
                vb = v_s[pl.ds(c0, C), :]
                rows = c0 + lax.broadcasted_iota(jnp.int32, (C, 1), 0)
                decay = jnp.exp(jnp.where(rows <= n, bn - bb, -jnp.inf))
                a = jnp.sum(qn * decay * kb, axis=-1, keepdims=True)
                oi_s[h, pl.ds(n, 1), :] = jnp.sum(a * vb, axis=0, keepdims=True)
                return carry
            lax.fori_loop(0, rb, row, 0)

    for h in range(hp):
        hs = slice(h * DK, (h + 1) * DK)
        o = oi_s[h] + oc_s[h]
        o_ref[:, hs] = (_rms(o, ng_ref[...]) * _sigmoid(g_ref[:, hs])).astype(o_ref.dtype)


def _hgrn(proj, lb_logits, norm_g, layer_j, B, S, rb=256, hp=4):
    T = B * S
    nrb = S // rb
    H = HGRN_HEADS
    ng = H // hp
    n_layers = lb_logits.shape[0]
    w = hp * HGRN_DK
    blk = lambda c: pl.BlockSpec((rb, w), lambda b, h, r, c=c: (b * nrb + r, c * ng + h))
    vm = lambda *lead: pltpu.VMEM((*lead, rb, HGRN_DK), F32)
    return pl.pallas_call(
        functools.partial(_hgrn_kernel, rb=rb, hp=hp, layer_j=layer_j),
        grid=(B, ng, nrb),
        in_specs=[blk(0), blk(1), blk(2), blk(3),
                  pl.BlockSpec((n_layers, w), lambda b, h, r: (0, h)),
                  pl.BlockSpec((1, HGRN_DV), lambda b, h, r: (0, 0))],
        out_specs=pl.BlockSpec((rb, w), lambda b, h, r: (b * nrb + r, h)),
        out_shape=jax.ShapeDtypeStruct((T, H * HGRN_DV), BF16),
        scratch_shapes=[pltpu.VMEM((hp, HGRN_DV, HGRN_DK), F32), vm(hp), vm(hp), vm(), vm(), vm(), vm()],
        compiler_params=pltpu.CompilerParams(
            dimension_semantics=("parallel", "parallel", "arbitrary"), vmem_limit_bytes=VMEM_LIMIT),
        name="hgrn2",
    )(proj, proj, proj, proj, lb_logits, norm_g.reshape(1, HGRN_DV))


def _mix_ffn_kernel(*refs, n_mix, final):
    x_ref = refs[0]
    mix_refs = refs[1:1 + n_mix]
    wo_refs = refs[1 + n_mix:1 + 2 * n_mix]
    g_ref, w1_ref, w3_ref, w2_ref = refs[1 + 2 * n_mix:5 + 2 * n_mix]
    o_ref = refs[-1]
    x1 = x_ref[...]
    for m_ref, wo_ref in zip(mix_refs, wo_refs):
        x1 = x1 + _dot(m_ref[...], wo_ref[...])
    h = _rms(x1, g_ref[...]).astype(BF16)
    a = _dot(h, w1_ref[...])
    u = (a * _sigmoid(a) * _dot(h, w3_ref[...])).astype(BF16)
    y = x1 + _dot(u, w2_ref[...])
    if final:
        y = _rms(y, refs[-2][...])
    o_ref[...] = y


def _mix_ffn(x2, mixes, wos, g, w1, w3, w2, final_g=None, tm=256):
    T, D = x2.shape
    assert T % tm == 0
    n_mix = len(mixes)
    const = lambda shape: pl.BlockSpec(shape, lambda i: (0,) * len(shape), pipeline_mode=pl.Buffered(1))
    in_specs = [pl.BlockSpec((tm, D), lambda i: (i, 0))]
    in_specs += [pl.BlockSpec((tm, m.shape[1]), lambda i: (i, 0)) for m in mixes]
    in_specs += [const(w.shape) for w in wos]
    in_specs += [const((1, D)), const(w1.shape), const(w3.shape), const(w2.shape)]
    args = [x2, *mixes, *wos, g.reshape(1, D), w1, w3, w2]
    if final_g is not None:
        in_specs.append(const((1, D)))
        args.append(final_g.reshape(1, D))
    return pl.pallas_call(
        functools.partial(_mix_ffn_kernel, n_mix=n_mix, final=final_g is not None),
        grid=(T // tm,),
        in_specs=in_specs,
        out_specs=pl.BlockSpec((tm, D), lambda i: (i, 0)),
        out_shape=jax.ShapeDtypeStruct((T, D), F32),
        compiler_params=pltpu.CompilerParams(
            dimension_semantics=("parallel",), vmem_limit_bytes=VMEM_LIMIT),
        name="mix_ffn",
    )(*args)


def _even_w_in_layout(w):
    D = w.shape[0]
    hd, G = NSA_HD, NSA_KV_GROUPS
    o_kc = 4 * RET_W + NSA_HEADS * hd
    kv = lambda idx, g: w[:, o_kc + idx * G * hd + g * hd:o_kc + idx * G * hd + (g + 1) * hd]
    o_ng = o_kc + 6 * G * hd
    cols = [w[:, :o_kc]]
    for idx_k, idx_v in ((2, 3), (4, 5)):
        for g in range(G):
            cols += [kv(idx_k, g), kv(idx_v, g)]
    for g in range(G):
        gate = [w[:, o_ng + c * NSA_HEADS + g * NSA_HPG:o_ng + c * NSA_HEADS + (g + 1) * NSA_HPG]
                for c in range(3)]
        cols += gate + [jnp.zeros((D, LANES - 3 * NSA_HPG), w.dtype)]
    for g in range(G):
        cols += [kv(0, g), kv(1, g)]
    out = jnp.concatenate(cols, axis=1)
    assert out.shape[1] == EVEN_N
    return out.astype(BF16)


def _compress_weights(pos_k, w1_k, w2_k, pos_v, w1_v, w2_v):
    hd = NSA_HD
    z = jnp.zeros((CMP_BLOCK, hd, hd), F32)
    w1k = w1_k.reshape(CMP_BLOCK, hd, hd)
    w1v = w1_v.reshape(CMP_BLOCK, hd, hd)
    top = jnp.concatenate([w1k, z], axis=2)
    bot = jnp.concatenate([z, w1v], axis=2)
    w1 = jnp.concatenate([top, bot], axis=1)
    w1 = w1.reshape(2, CMP_STRIDE * LANES, LANES).astype(BF16)
    pos = jnp.concatenate([pos_k, pos_v], axis=1).reshape(2, 1, CMP_STRIDE * LANES)
    zz = jnp.zeros((hd, hd), F32)
    w2 = jnp.concatenate([jnp.concatenate([w2_k, zz], axis=1),
                          jnp.concatenate([zz, w2_v], axis=1)], axis=0).astype(BF16)
    return pos, w1, w2


def _rotary_tables(S):
    half = RET_DK // 2
    inv = ROPE_BASE ** (-jnp.arange(half, dtype=F32) / half)
    ang = jnp.arange(S, dtype=F32)[:, None] * inv[None, :]
    cos, sin = jnp.cos(ang), jnp.sin(ang)
    return jnp.concatenate([cos, cos], axis=1), jnp.concatenate([-sin, sin], axis=1)


def kernel(x, norm_mix_g, norm_ffn_g, final_norm_g, even_w_in, even_w_out, cmp_pos_k, cmp_w1_k, cmp_w2_k, cmp_pos_v, cmp_w1_v, cmp_w2_v, odd_w_in, odd_w_out, hgrn_norm_g, hgrn_lb_logits, ffn_w1, ffn_w3, ffn_w2):
    B, S, D = x.shape
    x2 = x.reshape(B * S, D)
    cosf, sinf = _rotary_tables(S)
    for layer in range(DEPTH):
        j = layer // 2
        if layer % 2 == 0:
            w_in = _even_w_in_layout(even_w_in[j])
            main, cmp0, cmp1 = _norm_proj(x2, norm_mix_g[layer], w_in, (EVEN_MAIN, LANES, LANES))
            pos, cw1, cw2 = _compress_weights(cmp_pos_k[j], cmp_w1_k[j], cmp_w2_k[j],
                                              cmp_pos_v[j], cmp_w1_v[j], cmp_w2_v[j])
            kvcmp = _compress(cmp0, cmp1, pos, cw1, cw2, B, S)
            o_r = _retention(main, cosf, sinf, B, S)
            o_n = _nsa(main, kvcmp, B, S)
            w_out = even_w_out[j].astype(BF16)
            mixes, wos = (o_r, o_n), (w_out[:RET_W], w_out[RET_W:])
        else:
            (proj,) = _norm_proj(x2, norm_mix_g[layer], odd_w_in[j].astype(BF16), (4 * D,))
            o_h = _hgrn(proj, hgrn_lb_logits, hgrn_norm_g[j], j, B, S)
            mixes, wos = (o_h,), (odd_w_out[j].astype(BF16),)
        x2 = _mix_ffn(x2, mixes, wos, norm_ffn_g[layer],
                      ffn_w1[layer].astype(BF16), ffn_w3[layer].astype(BF16), ffn_w2[layer].astype(BF16),
                      final_g=final_norm_g if layer == DEPTH - 1 else None)
    return x2.reshape(B, S, D)
```

```python
import functools
import math

import jax
import jax.numpy as jnp
from jax import lax
from jax.experimental import pallas as pl
from jax.experimental.pallas import tpu as pltpu

F32 = jnp.float32
BF16 = jnp.bfloat16

D_MODEL = 1024
DEPTH = 4
RET_HEADS = 4
RET_DK = 128
RET_DV = 128
ROPE_BASE = 10000.0
NSA_HEADS = 8
NSA_KV_GROUPS = 2
NSA_HPG = NSA_HEADS // NSA_KV_GROUPS
NSA_HD = 64
CMP_BLOCK = 32
CMP_STRIDE = 16
SLC_BLOCK = 64
N_SELECT = 8
FORCED_LOCAL = 2
FORCE_BONUS = 1.0e4
WINDOW = 512
Q_BLOCK = 128
HGRN_HEADS = 8
HGRN_DK = D_MODEL // HGRN_HEADS
HGRN_DV = D_MODEL // HGRN_HEADS
HGRN_CHUNK = 64
D_FF = -(-8 * D_MODEL // (3 * 256)) * 256
EPS = 1e-6
EVEN_MIX = RET_HEADS * RET_DV + NSA_HEADS * NSA_HD

LANES = 128
VMEM_LIMIT = 56 * 1024 * 1024
NEG = -1e30
M_FLOOR = -1e29
LOG2E = math.log2(math.e)

RET_W = RET_HEADS * RET_DK
COL_RQ, COL_RK, COL_RV, COL_RG = 0, RET_W, 2 * RET_W, 3 * RET_W
COL_NQ = 4 * RET_W
NQ_W = NSA_HPG * NSA_HD
COL_SLC = COL_NQ + NSA_KV_GROUPS * NQ_W
COL_WIN = COL_SLC + NSA_KV_GROUPS * LANES
COL_GATE = COL_WIN + NSA_KV_GROUPS * LANES
EVEN_MAIN = COL_GATE + NSA_KV_GROUPS * LANES
EVEN_N = EVEN_MAIN + NSA_KV_GROUPS * LANES


def _dot(a, b):
    return jnp.dot(a, b, preferred_element_type=F32)


def _dot_nt(a, b):
    return lax.dot_general(a, b, (((1,), (1,)), ((), ())), preferred_element_type=F32)


def _dot_tn(a, b):
    return lax.dot_general(a, b, (((0,), (0,)), ((), ())), preferred_element_type=F32)


def _split3(x):
    hi = x.astype(BF16)
    r = x - hi.astype(F32)
    mid = r.astype(BF16)
    lo = (r - mid.astype(F32)).astype(BF16)
    return hi, mid, lo


def _dot_exact_lhs(sel, x):
    hi, mid, lo = _split3(x)
    return _dot(sel, hi) + _dot(sel, mid) + _dot(sel, lo)


def _dot_exact_rhs(x, sel):
    hi, mid, lo = _split3(x)
    return _dot(hi, sel) + _dot(mid, sel) + _dot(lo, sel)


def _rms(x, g):
    return x * lax.rsqrt(jnp.mean(x * x, axis=-1, keepdims=True) + EPS) * g


def _sigmoid(x):
    return 1.0 / (1.0 + jnp.exp(-x))


def _norm_proj_kernel(x_ref, g_ref, w_ref, *o_refs):
    h = _rms(x_ref[...], g_ref[...]).astype(BF16)
    off = 0
    for o_ref in o_refs:
        wd = o_ref.shape[1]
        o_ref[...] = _dot(h, w_ref[:, off:off + wd])
        off += wd


def _layer_spec(stack, layer, rows=None, row_block=0):
    _, r, c = stack.shape
    rows = r if rows is None else rows
    return pl.BlockSpec((None, rows, c), lambda *_: (layer, row_block, 0), pipeline_mode=pl.Buffered(1))


def _norm_proj(x2, gains, layer, w_stack, j, widths, tm=512):
    T, D = x2.shape
    N = w_stack.shape[2]
    assert sum(widths) == N and T % tm == 0
    return pl.pallas_call(
        _norm_proj_kernel,
        grid=(T // tm,),
        in_specs=[pl.BlockSpec((tm, D), lambda i: (i, 0)), _layer_spec(gains, layer), _layer_spec(w_stack, j)],
        out_specs=[pl.BlockSpec((tm, wd), lambda i: (i, 0)) for wd in widths],
        out_shape=[jax.ShapeDtypeStruct((T, wd), F32) for wd in widths],
        compiler_params=pltpu.CompilerParams(
            dimension_semantics=("parallel",), vmem_limit_bytes=VMEM_LIMIT),
        name="norm_proj",
    )(x2, gains, w_stack)


def _retention_kernel(q_ref, k_ref, v_ref, g_ref, cos_ref, sin_ref, o_ref, state_ref, *, cb):
    @pl.when(pl.program_id(1) == 0)
    def _():
        state_ref[...] = jnp.zeros_like(state_ref)

    cosf = cos_ref[...]
    sinf = sin_ref[...]
    ri = lax.broadcasted_iota(jnp.int32, (cb, cb), 0)
    ci = lax.broadcasted_iota(jnp.int32, (cb, cb), 1)
    diff = (ri - ci).astype(F32)
    causal = ri >= ci
    idx = lax.broadcasted_iota(jnp.int32, (cb, 1), 0).astype(F32)
    for h in range(RET_HEADS):
        lg = math.log(1.0 - 2.0 ** (-5.0 - h))
        sl = slice(h * RET_DK, (h + 1) * RET_DK)
        q = q_ref[:, sl]
        k = k_ref[:, sl]
        v = v_ref[:, sl].astype(BF16)
        qr = q * cosf + pltpu.roll(q, RET_DK // 2, 1) * sinf
        kr = (k * cosf + pltpu.roll(k, RET_DK // 2, 1) * sinf) * (RET_DK ** -0.5)
        dmat = jnp.where(causal, jnp.exp(lg * jnp.where(causal, diff, 0.0)), 0.0)
        scores = _dot_nt(qr.astype(BF16), kr.astype(BF16)) * dmat
        o = _dot(scores.astype(BF16), v)
        q_dec = qr * jnp.exp(lg * (idx + 1.0))
        state = state_ref[h]
        o = o + _dot(q_dec.astype(BF16), state.astype(BF16))
        k_dec = kr * jnp.exp(lg * (cb - 1.0 - idx))
        state_ref[h] = math.exp(lg * cb) * state + _dot_tn(k_dec.astype(BF16), v)
        mu = jnp.mean(o, axis=-1, keepdims=True)
        oc = o - mu
        var = jnp.mean(oc * oc, axis=-1, keepdims=True)
        gate = g_ref[:, sl]
        o_ref[:, sl] = (oc * lax.rsqrt(var + 1e-5) * (gate * _sigmoid(gate))).astype(o_ref.dtype)


def _retention(main, cosf, sinf, B, S, cb=256):
    T = B * S
    nc = S // cb
    blk = lambda c: pl.BlockSpec((cb, RET_W), lambda b, i, c=c: (b * nc + i, c))
    tab = pl.BlockSpec((cb, RET_DK), lambda b, i: (i, 0))
    return pl.pallas_call(
        functools.partial(_retention_kernel, cb=cb),
        grid=(B, nc),
        in_specs=[blk(COL_RQ // RET_W), blk(COL_RK // RET_W), blk(COL_RV // RET_W),
                  blk(COL_RG // RET_W), tab, tab],
        out_specs=pl.BlockSpec((cb, RET_W), lambda b, i: (b * nc + i, 0)),
        out_shape=jax.ShapeDtypeStruct((T, RET_W), BF16),
        scratch_shapes=[pltpu.VMEM((RET_HEADS, RET_DK, RET_DV), F32)],
        compiler_params=pltpu.CompilerParams(
            dimension_semantics=("parallel", "arbitrary"), vmem_limit_bytes=VMEM_LIMIT),
        name="retention",
    )(main, main, main, main, cosf, sinf)


def _gelu_tanh(y):
    return 0.5 * y * (1.0 + jnp.tanh(math.sqrt(2.0 / math.pi) * (y + 0.044715 * (y * y * y))))


def _compress_kernel(r0_ref, r1_ref, pos_ref, w1_ref, w2_ref, o_ref):
    for g, r_ref in enumerate((r0_ref, r1_ref)):
        r = r_ref[0]
        n = r.shape[0]
        y_lo = _dot((r + pos_ref[0]).astype(BF16), w1_ref[0])
        y_hi = _dot((r + pos_ref[1]).astype(BF16), w1_ref[1])
        y = y_lo + pltpu.roll(y_hi, n - 1, 0)
        out = _dot(_gelu_tanh(y).astype(BF16), w2_ref[...])
        row = lax.broadcasted_iota(jnp.int32, out.shape, 0)
        o_ref[0, g] = jnp.where(row < n - 1, out, 0.0)


def _compress(cmp0, cmp1, pos, w1, w2, B, S):
    nr = S // CMP_STRIDE
    kw = CMP_STRIDE * LANES
    r0 = cmp0.reshape(B, nr, kw)
    r1 = cmp1.reshape(B, nr, kw)
    rspec = pl.BlockSpec((1, nr, kw), lambda b: (b, 0, 0))
    return pl.pallas_call(
        _compress_kernel,
        grid=(B,),
        in_specs=[rspec, rspec,
                  pl.BlockSpec((2, 1, kw), lambda b: (0, 0, 0)),
                  pl.BlockSpec((2, kw, LANES), lambda b: (0, 0, 0)),
                  pl.BlockSpec((LANES, LANES), lambda b: (0, 0))],
        out_specs=pl.BlockSpec((1, NSA_KV_GROUPS, nr, LANES), lambda b: (b, 0, 0, 0)),
        out_shape=jax.ShapeDtypeStruct((B, NSA_KV_GROUPS, nr, LANES), F32),
        compiler_params=pltpu.CompilerParams(
            dimension_semantics=("parallel",), vmem_limit_bytes=VMEM_LIMIT),
        name="nsa_compress",
    )(r0, r1, pos, w1, w2)


NSA_WBLK = LANES
NSA_VPAD = 16


def _tile_heads(x):
    return jnp.concatenate([x] * NSA_HPG, axis=1)


def _nsa_kernel(q_ref, cmp_ref, slc_ref, win_ref, gate_ref, expand_ref, o_ref,
                ks_ref, vst_ref, kw_ref, vwt_ref, m_ref, acc_ref, *, S, kc):
    QT = q_ref.shape[0]
    qb = pl.program_id(2)
    q0 = qb * QT
    n_slc = S // SLC_BLOCK
    n_cmp = S // CMP_STRIDE
    hd, H = NSA_HD, NSA_HPG

    @pl.when(qb == 0)
    def _():
        slab = slc_ref[...]
        ks_ref[:, :hd] = slab[:, :hd].astype(BF16)
        ks_ref[:, hd:] = expand_ref[...]
        ones_rows = lambda n: jnp.where(lax.broadcasted_iota(jnp.int32, (NSA_VPAD, n), 0) == 0, 1.0, 0.0)
        for c in range(S // kc):
            vt = slab[c * kc:(c + 1) * kc].T[hd:]
            vst_ref[c] = jnp.concatenate([vt, ones_rows(kc)], axis=0).astype(BF16)
        slab = win_ref[...]
        kw_ref[...] = slab[:, :hd].astype(BF16)
        for c in range(S // NSA_WBLK):
            vt = slab[c * NSA_WBLK:(c + 1) * NSA_WBLK].T[hd:]
            vwt_ref[c] = jnp.concatenate([vt, ones_rows(NSA_WBLK)], axis=0).astype(BF16)

    q_t = (q_ref[...] * (hd ** -0.5)).T
    q_f = jnp.concatenate([q_t[h * hd:(h + 1) * hd] for h in range(H)], axis=1)
    q_cat = q_f.astype(BF16)
    q_cat2 = (q_f * LOG2E).astype(BF16)
    t_row = q0 + lax.broadcasted_iota(jnp.int32, (1, QT), 1)

    wlen = WINDOW + QT
    w0 = pl.multiple_of(jnp.maximum(q0 - WINDOW, 0), NSA_WBLK)
    dist = t_row - (w0 + lax.broadcasted_iota(jnp.int32, (wlen, 1), 0))
    bias_w = jnp.where(dist >= 0, jnp.where(dist < WINDOW, 0.0, NEG), NEG)
    s = _dot(kw_ref[pl.ds(w0, wlen), :], q_cat2) + _tile_heads(bias_w)
    e = jnp.exp2((s - jnp.max(s, axis=0, keepdims=True)).astype(BF16))
    wb = w0 // NSA_WBLK
    o_w = _dot(vwt_ref[wb], e[:NSA_WBLK])
    for j in range(1, wlen // NSA_WBLK):
        o_w = o_w + _dot(vwt_ref[wb + j], e[j * NSA_WBLK:(j + 1) * NSA_WBLK])
    o_w = o_w[:hd] * (1.0 / jnp.maximum(o_w[hd:hd + 1], 1e-30))

    cmp_t = cmp_ref[0, 0].T
    k_cmp = cmp_ref[0, 0, :, :hd].astype(BF16)
    cmp_end = lax.broadcasted_iota(jnp.int32, (n_cmp, 1), 0) * CMP_STRIDE + (CMP_BLOCK - 1)
    s = _dot(k_cmp, q_cat) + _tile_heads(jnp.where(cmp_end <= t_row, 0.0, NEG))
    e = jnp.exp(s - jnp.max(s, axis=0, keepdims=True))
    any_c = _tile_heads(jnp.where(t_row >= CMP_BLOCK - 1, 1.0, 0.0))
    p_c = e * (any_c / jnp.maximum(jnp.sum(e, axis=0, keepdims=True), 1e-30))
    o_c = _dot(cmp_t[hd:].astype(BF16), p_c.astype(BF16))
    p_sum = p_c[:, :QT]
    for h in range(1, H):
        p_sum = p_sum + p_c[:, h * QT:(h + 1) * QT]

    sj = lax.broadcasted_iota(jnp.int32, (n_slc, n_cmp), 0) * SLC_BLOCK
    cn = lax.broadcasted_iota(jnp.int32, (n_slc, n_cmp), 1) * CMP_STRIDE
    overlap = jnp.where(cn <= sj + (SLC_BLOCK - 1),
                        jnp.where(cn + (CMP_BLOCK - 1) >= sj, 1.0, 0.0), 0.0).astype(BF16)
    imp = _dot_exact_lhs(overlap, p_sum)
    jblk = lax.broadcasted_iota(jnp.int32, (n_slc, QT), 0)
    bt = (q0 + lax.broadcasted_iota(jnp.int32, (n_slc, QT), 1)) // SLC_BLOCK
    back = bt - jblk
    bonus = jnp.where(jblk == 0, FORCE_BONUS,
                      jnp.where(back >= 0, jnp.where(back < FORCED_LOCAL, FORCE_BONUS, 0.0), 0.0))
    score = jnp.where(jblk <= bt, imp + bonus, NEG)
    SUB = 8
    groups = [score[r:r + SUB] for r in range(0, n_slc, SUB)]
    ranks = [jnp.zeros((SUB, QT), F32) for _ in groups]
    for i in range(n_slc):
        row = score[i:i + 1, :]
        for gi, grp in enumerate(groups):
            lo = gi * SUB
            if lo > i:
                ahead = row >= grp
            elif lo + SUB - 1 <= i:
                ahead = row > grp
            else:
                below = lo + lax.broadcasted_iota(jnp.int32, (SUB, QT), 0) > i
                ahead = jnp.where(below, jnp.where(row >= grp, 1.0, 0.0),
                                  jnp.where(row > grp, 1.0, 0.0)) > 0.5
            ranks[gi] = ranks[gi] + jnp.where(ahead, 1.0, 0.0)
    rank = jnp.concatenate(ranks, axis=0)
    sel_bias = jnp.where(rank < float(min(N_SELECT, n_slc)), 0.0, NEG).astype(BF16)

    ns = m_ref.shape[0]
    m_ref[...] = jnp.full_like(m_ref, M_FLOOR)
    acc_ref[...] = jnp.zeros_like(acc_ref)
    q_aug = jnp.concatenate([q_cat2, _tile_heads(sel_bias)], axis=0)
    gap = (q0 + lax.broadcasted_iota(jnp.int32, (kc, QT), 1)) - lax.broadcasted_iota(jnp.int32, (kc, QT), 0)

    def chunks(it, causal_mask):
        scores = []
        for j in range(ns):
            c = it * ns + j
            s = _dot(ks_ref[pl.ds(pl.multiple_of(c * kc, kc), kc), :], q_aug)
            if causal_mask:
                s = s + _tile_heads(jnp.where(gap >= c * kc, 0.0, NEG))
            scores.append(s)
        for j in range(ns):
            c = it * ns + j
            s = scores[j]
            m_old = m_ref[j]
            m_new = jnp.maximum(m_old, jnp.max(s, axis=0, keepdims=True))
            alpha = jnp.exp2(m_old - m_new)
            p = jnp.exp2((s - m_new).astype(BF16))
            acc_ref[j] = alpha * acc_ref[j] + _dot(vst_ref[c], p)
            m_ref[j] = m_new

    n_full = q0 // (ns * kc)
    lax.fori_loop(0, n_full, lambda it, carry: (chunks(it, False), carry)[1], 0)
    chunks(n_full, True)
    m_all = m_ref[0]
    for j in range(1, ns):
        m_all = jnp.maximum(m_all, m_ref[j])
    acc = jnp.zeros(acc_ref.shape[1:], F32)
    for j in range(ns):
        acc = acc + jnp.exp2(m_ref[j] - m_all) * acc_ref[j]
    o_s = acc[:hd] * (1.0 / jnp.maximum(acc[hd:hd + 1], 1e-30))

    sig_t = _sigmoid(gate_ref[...]).T
    gate = lambda c: jnp.concatenate([sig_t[c * H + h:c * H + h + 1] for h in range(H)], axis=1)
    out_t = gate(0) * o_c + gate(1) * o_s + gate(2) * o_w
    out = jnp.concatenate([out_t[:, h * QT:(h + 1) * QT] for h in range(H)], axis=0)
    o_ref[...] = out.T.astype(o_ref.dtype)


def _nsa(main, kvcmp, B, S, kc=128, qt=256, ns=4):
    T = B * S
    nqb = S // qt
    G = NSA_KV_GROUPS
    assert S % (ns * kc) == 0 and (ns * kc) % qt == 0 and qt % NSA_WBLK == 0 and S >= WINDOW + qt
    n_slc = S // SLC_BLOCK
    expand = (jnp.arange(S, dtype=jnp.int32)[:, None] // SLC_BLOCK
              == jnp.arange(n_slc, dtype=jnp.int32)[None, :]).astype(BF16)
    return pl.pallas_call(
        functools.partial(_nsa_kernel, S=S, kc=kc),
        grid=(B, G, nqb),
        in_specs=[
            pl.BlockSpec((qt, NQ_W), lambda b, g, i: (b * nqb + i, COL_NQ // NQ_W + g)),
            pl.BlockSpec((1, 1, S // CMP_STRIDE, LANES), lambda b, g, i: (b, g, 0, 0)),
            pl.BlockSpec((S, LANES), lambda b, g, i: (b, COL_SLC // LANES + g)),
            pl.BlockSpec((S, LANES), lambda b, g, i: (b, COL_WIN // LANES + g)),
            pl.BlockSpec((qt, LANES), lambda b, g, i: (b * nqb + i, COL_GATE // LANES + g)),
            pl.BlockSpec((S, n_slc), lambda b, g, i: (0, 0)),
        ],
        out_specs=pl.BlockSpec((qt, NQ_W), lambda b, g, i: (b * nqb + i, g)),
        out_shape=jax.ShapeDtypeStruct((T, G * NQ_W), BF16),
        scratch_shapes=[
            pltpu.VMEM((S, NSA_HD + n_slc), BF16),
            pltpu.VMEM((S // kc, NSA_HD + NSA_VPAD, kc), BF16),
            pltpu.VMEM((S, NSA_HD), BF16),
            pltpu.VMEM((S // NSA_WBLK, NSA_HD + NSA_VPAD, NSA_WBLK), BF16),
            pltpu.VMEM((ns, 1, NSA_HPG * qt), F32),
            pltpu.VMEM((ns, NSA_HD + NSA_VPAD, NSA_HPG * qt), F32)],
        compiler_params=pltpu.CompilerParams(
            dimension_semantics=("parallel", "parallel", "arbitrary"), vmem_limit_bytes=VMEM_LIMIT),
        name="nsa_attention",
    )(main, kvcmp, main, main, main, expand)


HGRN_SUB = HGRN_CHUNK // 2
HGRN_SAFE_RANGE = 60.0


def _hgrn_kernel(q_ref, f_ref, i_ref, g_ref, lbl_ref, ng_ref, o_ref,
                 st_ref, oi_s, oc_s, q_s, k_s, b_s, v_s, *, rb, hp, layer_j):
    C, SB, DK = HGRN_CHUNK, HGRN_SUB, HGRN_DK
    nch = rb // C

    @pl.when(pl.program_id(2) == 0)
    def _():
        st_ref[...] = jnp.zeros_like(st_ref)

    logits = lbl_ref[...]
    e = jnp.exp(logits - jnp.max(logits, axis=0, keepdims=True))
    sm = e / jnp.sum(e, axis=0, keepdims=True)
    lb_all = jnp.maximum(jnp.sum(sm[:layer_j + 1], axis=0, keepdims=True) - sm[0:1], 0.0)

    ri = lax.broadcasted_iota(jnp.int32, (rb, rb), 0)
    ci = lax.broadcasted_iota(jnp.int32, (rb, rb), 1)
    lower = ((ri // C) == (ci // C)) & (ci <= ri)
    same_sub = (ri // SB) == (ci // SB)
    ltri = jnp.where(lower, 1.0, 0.0).astype(BF16)
    cat = lambda xs: jnp.concatenate(xs, axis=0).astype(BF16)

    heads = []
    span = jnp.zeros((1, DK), F32)
    for h in range(hp):
        hs = slice(h * DK, (h + 1) * DK)
        lb = lb_all[:, hs]
        fp = f_ref[:, hs]
        qp = q_ref[:, hs]
        v = i_ref[:, hs]
        log_sig = jnp.minimum(fp, 0.0) - jnp.log(1.0 + jnp.exp(-jnp.abs(fp)))
        c_term = jnp.log1p(-lb) + log_sig
        a_term = jnp.log(lb)
        log_f = jnp.maximum(a_term, c_term) + jnp.log(1.0 + jnp.exp(-jnp.abs(a_term - c_term)))
        k = (1.0 - lb) * _sigmoid(-fp)
        q = qp * _sigmoid(qp) * (DK ** -0.5)
        b = _dot_exact_lhs(ltri, log_f)
        for s0 in range(0, rb, SB):
            span = jnp.maximum(span, b[s0:s0 + 1] - b[s0 + SB - 1:s0 + SB])

        qd, kd, qo, ko = [], [], [], []
        for s0 in range(0, rb, SB):
            bb = b[s0:s0 + SB]
            ref = b[s0 + SB // 2:s0 + SB // 2 + 1]
            qd.append(q[s0:s0 + SB] * jnp.exp(bb - ref))
            kd.append(k[s0:s0 + SB] * jnp.exp(ref - bb))
            if (s0 // SB) % 2 == 0:
                end = b[s0 + SB - 1:s0 + SB]
                ko.append(k[s0:s0 + SB] * jnp.exp(end - bb))
                qo.append(jnp.zeros((SB, DK), F32))
            else:
                end = b[s0 - 1:s0]
                qo.append(q[s0:s0 + SB] * jnp.exp(bb - end))
                ko.append(jnp.zeros((SB, DK), F32))
        a_diag = _dot_nt(cat(qd), cat(kd))
        a_off = _dot_nt(cat(qo), cat(ko))
        attn = jnp.where(lower, jnp.where(same_sub, a_diag, a_off), 0.0)
        vb16 = v.astype(BF16)
        oi_s[h] = _dot(attn.astype(BF16), vb16)

        qb = (q * jnp.exp(b)).astype(BF16)
        st = st_ref[h]
        outs = []
        for c in range(nch):
            sl = slice(c * C, (c + 1) * C)
            b_last = b[(c + 1) * C - 1:(c + 1) * C]
            outs.append(_dot_nt(qb[sl], st.astype(BF16)))
            k_dec = (k[sl] * jnp.exp(b_last - b[sl])).astype(BF16)
            st = jnp.exp(b_last) * st + _dot_tn(vb16[sl], k_dec)
        st_ref[h] = st
        oc_s[h] = jnp.concatenate(outs, axis=0)
        heads.append((q, k, b, v))

    @pl.when(jnp.max(span) > HGRN_SAFE_RANGE)
    def _():
        for h, (q, k, b, v) in enumerate(heads):
            q_s[...] = q
            k_s[...] = k
            b_s[...] = b
            v_s[...] = v

            def row(n, carry):
                c0 = pl.multiple_of((n // C) * C, C)
                qn = q_s[pl.ds(n, 1), :]
                bn = b_s[pl.ds(n, 1), :]
                kb = k_s[pl.ds(c0, C), :]
                bb = b_s[pl.ds(c0, C), :]
                vb = v_s[pl.ds(c0, C), :]
                rows = c0 + lax.broadcasted_iota(jnp.int32, (C, 1), 0)
                decay = jnp.exp(jnp.where(rows <= n, bn - bb, -jnp.inf))
                a = jnp.sum(qn * decay * kb, axis=-1, keepdims=True)
                oi_s[h, pl.ds(n, 1), :] = jnp.sum(a * vb, axis=0, keepdims=True)
                return carry
            lax.fori_loop(0, rb, row, 0)

    for h in range(hp):
        hs = slice(h * DK, (h + 1) * DK)
        o = oi_s[h] + oc_s[h]
        o_ref[:, hs] = (_rms(o, ng_ref[...]) * _sigmoid(g_ref[:, hs])).astype(o_ref.dtype)


def _hgrn(proj, lb_logits, norm_g, layer_j, B, S, rb=256, hp=8):
    T = B * S
    nrb = S // rb
    H = HGRN_HEADS
    ng = H // hp
    n_layers = lb_logits.shape[0]
    w = hp * HGRN_DK
    blk = lambda c: pl.BlockSpec((rb, w), lambda b, h, r, c=c: (b * nrb + r, c * ng + h))
    vm = lambda *lead: pltpu.VMEM((*lead, rb, HGRN_DK), F32)
    return pl.pallas_call(
        functools.partial(_hgrn_kernel, rb=rb, hp=hp, layer_j=layer_j),
        grid=(B, ng, nrb),
        in_specs=[blk(0), blk(1), blk(2), blk(3),
                  pl.BlockSpec((n_layers, w), lambda b, h, r: (0, h)),
                  pl.BlockSpec((1, HGRN_DV), lambda b, h, r: (0, 0))],
        out_specs=pl.BlockSpec((rb, w), lambda b, h, r: (b * nrb + r, h)),
        out_shape=jax.ShapeDtypeStruct((T, H * HGRN_DV), BF16),
        scratch_shapes=[pltpu.VMEM((hp, HGRN_DV, HGRN_DK), F32), vm(hp), vm(hp), vm(), vm(), vm(), vm()],
        compiler_params=pltpu.CompilerParams(
            dimension_semantics=("parallel", "parallel", "arbitrary"), vmem_limit_bytes=VMEM_LIMIT),
        name="hgrn2",
    )(proj, proj, proj, proj, lb_logits, norm_g.reshape(1, HGRN_DV))


def _mix_ffn_kernel(*refs, n_mix, final):
    x_ref = refs[0]
    mix_refs = refs[1:1 + n_mix]
    wo_refs = refs[1 + n_mix:1 + 2 * n_mix]
    g_ref, w1_ref, w3_ref, w2_ref = refs[1 + 2 * n_mix:5 + 2 * n_mix]
    o_ref = refs[-1]
    x1 = x_ref[...]
    for m_ref, wo_ref in zip(mix_refs, wo_refs):
        x1 = x1 + _dot(m_ref[...], wo_ref[...])
    h = _rms(x1, g_ref[...]).astype(BF16)
    a = _dot(h, w1_ref[...])
    u = (a * _sigmoid(a) * _dot(h, w3_ref[...])).astype(BF16)
    y = x1 + _dot(u, w2_ref[...])
    if final:
        y = _rms(y, refs[-2][...])
    o_ref[...] = y


def _mix_ffn(x2, mixes, wo_stack, j, gains, layer, w1s, w3s, w2s, final_g=None, tm=512):
    T, D = x2.shape
    assert T % tm == 0 and sum(m.shape[1] for m in mixes) == wo_stack.shape[1]
    n_mix = len(mixes)
    in_specs = [pl.BlockSpec((tm, D), lambda i: (i, 0))]
    in_specs += [pl.BlockSpec((tm, m.shape[1]), lambda i: (i, 0)) for m in mixes]
    rows = mixes[0].shape[1]
    assert all(m.shape[1] == rows for m in mixes)
    in_specs += [_layer_spec(wo_stack, j, rows, r) for r in range(n_mix)]
    in_specs += [_layer_spec(gains, layer), _layer_spec(w1s, layer), _layer_spec(w3s, layer),
                 _layer_spec(w2s, layer)]
    args = [x2, *mixes, *([wo_stack] * n_mix), gains, w1s, w3s, w2s]
    if final_g is not None:
        in_specs.append(pl.BlockSpec((1, D), lambda i: (0, 0)))
        args.append(final_g.reshape(1, D))
    return pl.pallas_call(
        functools.partial(_mix_ffn_kernel, n_mix=n_mix, final=final_g is not None),
        grid=(T // tm,),
        in_specs=in_specs,
        out_specs=pl.BlockSpec((tm, D), lambda i: (i, 0)),
        out_shape=jax.ShapeDtypeStruct((T, D), F32),
        compiler_params=pltpu.CompilerParams(
            dimension_semantics=("parallel",), vmem_limit_bytes=VMEM_LIMIT),
        name="mix_ffn",
    )(*args)


def _even_w_in_layout(w):
    hd, G = NSA_HD, NSA_KV_GROUPS
    o_kc = 4 * RET_W + NSA_HEADS * hd
    kv = lambda idx, g: w[..., o_kc + idx * G * hd + g * hd:o_kc + idx * G * hd + (g + 1) * hd]
    o_ng = o_kc + 6 * G * hd
    cols = [w[..., :o_kc]]
    for idx_k, idx_v in ((2, 3), (4, 5)):
        for g in range(G):
            cols += [kv(idx_k, g), kv(idx_v, g)]
    for g in range(G):
        gate = [w[..., o_ng + c * NSA_HEADS + g * NSA_HPG:o_ng + c * NSA_HEADS + (g + 1) * NSA_HPG]
                for c in range(3)]
        cols += gate + [jnp.zeros((*w.shape[:-1], LANES - 3 * NSA_HPG), w.dtype)]
    for g in range(G):
        cols += [kv(0, g), kv(1, g)]
    out = jnp.concatenate(cols, axis=-1)
    assert out.shape[-1] == EVEN_N
    return out.astype(BF16)


def _compress_weights(pos_k, w1_k, w2_k, pos_v, w1_v, w2_v):
    hd = NSA_HD
    z = jnp.zeros((CMP_BLOCK, hd, hd), F32)
    w1k = w1_k.reshape(CMP_BLOCK, hd, hd)
    w1v = w1_v.reshape(CMP_BLOCK, hd, hd)
    top = jnp.concatenate([w1k, z], axis=2)
    bot = jnp.concatenate([z, w1v], axis=2)
    w1 = jnp.concatenate([top, bot], axis=1)
    w1 = w1.reshape(2, CMP_STRIDE * LANES, LANES).astype(BF16)
    pos = jnp.concatenate([pos_k, pos_v], axis=1).reshape(2, 1, CMP_STRIDE * LANES)
    zz = jnp.zeros((hd, hd), F32)
    w2 = jnp.concatenate([jnp.concatenate([w2_k, zz], axis=1),
                          jnp.concatenate([zz, w2_v], axis=1)], axis=0).astype(BF16)
    return pos, w1, w2


def _rotary_tables(S):
    half = RET_DK // 2
    inv = ROPE_BASE ** (-jnp.arange(half, dtype=F32) / half)
    ang = jnp.arange(S, dtype=F32)[:, None] * inv[None, :]
    cos, sin = jnp.cos(ang), jnp.sin(ang)
    return jnp.concatenate([cos, cos], axis=1), jnp.concatenate([-sin, sin], axis=1)


def kernel(x, norm_mix_g, norm_ffn_g, final_norm_g, even_w_in, even_w_out, cmp_pos_k, cmp_w1_k, cmp_w2_k, cmp_pos_v, cmp_w1_v, cmp_w2_v, odd_w_in, odd_w_out, hgrn_norm_g, hgrn_lb_logits, ffn_w1, ffn_w3, ffn_w2):
    B, S, D = x.shape
    x2 = x.reshape(B * S, D)
    cosf, sinf = _rotary_tables(S)
    w_in_even = _even_w_in_layout(even_w_in)
    w_in_odd = odd_w_in.astype(BF16)
    w_out_even = even_w_out.astype(BF16)
    w_out_odd = odd_w_out.astype(BF16)
    w1s, w3s, w2s = ffn_w1.astype(BF16), ffn_w3.astype(BF16), ffn_w2.astype(BF16)
    g_mix = norm_mix_g.reshape(DEPTH, 1, D)
    g_ffn = norm_ffn_g.reshape(DEPTH, 1, D)
    for layer in range(DEPTH):
        j = layer // 2
        if layer % 2 == 0:
            main, cmp0, cmp1 = _norm_proj(x2, g_mix, layer, w_in_even, j, (EVEN_MAIN, LANES, LANES))
            pos, cw1, cw2 = _compress_weights(cmp_pos_k[j], cmp_w1_k[j], cmp_w2_k[j],
                                              cmp_pos_v[j], cmp_w1_v[j], cmp_w2_v[j])
            kvcmp = _compress(cmp0, cmp1, pos, cw1, cw2, B, S)
            o_r = _retention(main, cosf, sinf, B, S)
            o_n = _nsa(main, kvcmp, B, S)
            mixes, wo_stack = (o_r, o_n), w_out_even
        else:
            (proj,) = _norm_proj(x2, g_mix, layer, w_in_odd, j, (4 * D,))
            o_h = _hgrn(proj, hgrn_lb_logits, hgrn_norm_g[j], j, B, S)
            mixes, wo_stack = (o_h,), w_out_odd
        x2 = _mix_ffn(x2, mixes, wo_stack, j, g_ffn, layer, w1s, w3s, w2s,
                      final_g=final_norm_g if layer == DEPTH - 1 else None)
    return x2.reshape(B, S, D)
```

```python
import functools
import math

import jax
import jax.numpy as jnp
from jax import lax
from jax.experimental import pallas as pl
from jax.experimental.pallas import tpu as pltpu

F32 = jnp.float32
BF16 = jnp.bfloat16

D_MODEL = 1024
DEPTH = 4
RET_HEADS = 4
RET_DK = 128
RET_DV = 128
ROPE_BASE = 10000.0
NSA_HEADS = 8
NSA_KV_GROUPS = 2
NSA_HPG = NSA_HEADS // NSA_KV_GROUPS
NSA_HD = 64
CMP_BLOCK = 32
CMP_STRIDE = 16
SLC_BLOCK = 64
N_SELECT = 8
FORCED_LOCAL = 2
FORCE_BONUS = 1.0e4
WINDOW = 512
Q_BLOCK = 128
HGRN_HEADS = 8
HGRN_DK = D_MODEL // HGRN_HEADS
HGRN_DV = D_MODEL // HGRN_HEADS
HGRN_CHUNK = 64
D_FF = -(-8 * D_MODEL // (3 * 256)) * 256
EPS = 1e-6
EVEN_MIX = RET_HEADS * RET_DV + NSA_HEADS * NSA_HD

LANES = 128
VMEM_LIMIT = 56 * 1024 * 1024
NEG = -1e30
M_FLOOR = -1e29
LOG2E = math.log2(math.e)

RET_W = RET_HEADS * RET_DK
COL_RQ, COL_RK, COL_RV, COL_RG = 0, RET_W, 2 * RET_W, 3 * RET_W
COL_NQ = 4 * RET_W
NQ_W = NSA_HPG * NSA_HD
COL_SLC = COL_NQ + NSA_KV_GROUPS * NQ_W
COL_WIN = COL_SLC + NSA_KV_GROUPS * LANES
COL_GATE = COL_WIN + NSA_KV_GROUPS * LANES
EVEN_MAIN = COL_GATE + NSA_KV_GROUPS * LANES
EVEN_N = EVEN_MAIN + NSA_KV_GROUPS * LANES


def _dot(a, b):
    return jnp.dot(a, b, preferred_element_type=F32)


def _dot_nt(a, b):
    return lax.dot_general(a, b, (((1,), (1,)), ((), ())), preferred_element_type=F32)


def _dot_tn(a, b):
    return lax.dot_general(a, b, (((0,), (0,)), ((), ())), preferred_element_type=F32)


def _split3(x):
    hi = x.astype(BF16)
    r = x - hi.astype(F32)
    mid = r.astype(BF16)
    lo = (r - mid.astype(F32)).astype(BF16)
    return hi, mid, lo


def _dot_exact_lhs(sel, x):
    hi, mid, lo = _split3(x)
    return _dot(sel, hi) + _dot(sel, mid) + _dot(sel, lo)


def _dot_exact_rhs(x, sel):
    hi, mid, lo = _split3(x)
    return _dot(hi, sel) + _dot(mid, sel) + _dot(lo, sel)


def _rms(x, g):
    return x * lax.rsqrt(jnp.mean(x * x, axis=-1, keepdims=True) + EPS) * g


def _sigmoid(x):
    return 1.0 / (1.0 + jnp.exp(-x))


def _softplus_neg_abs(x):
    return jnp.log2(1.0 + jnp.exp2(jnp.abs(x) * (-LOG2E))) * math.log(2.0)


def _norm_proj_kernel(x_ref, g_ref, w_ref, *o_refs):
    h = _rms(x_ref[...], g_ref[...]).astype(BF16)
    off = 0
    for o_ref in o_refs:
        wd = o_ref.shape[1]
        o_ref[...] = _dot(h, w_ref[:, off:off + wd])
        off += wd


def _layer_spec(stack, layer, rows=None, row_block=0):
    _, r, c = stack.shape
    rows = r if rows is None else rows
    return pl.BlockSpec((None, rows, c), lambda *_: (layer, row_block, 0), pipeline_mode=pl.Buffered(1))


def _norm_proj(x2, gains, layer, w_stack, j, widths, tm=512):
    T, D = x2.shape
    N = w_stack.shape[2]
    assert sum(widths) == N and T % tm == 0
    return pl.pallas_call(
        _norm_proj_kernel,
        grid=(T // tm,),
        in_specs=[pl.BlockSpec((tm, D), lambda i: (i, 0)), _layer_spec(gains, layer), _layer_spec(w_stack, j)],
        out_specs=[pl.BlockSpec((tm, wd), lambda i: (i, 0)) for wd in widths],
        out_shape=[jax.ShapeDtypeStruct((T, wd), F32) for wd in widths],
        compiler_params=pltpu.CompilerParams(
            dimension_semantics=("parallel",), vmem_limit_bytes=VMEM_LIMIT),
        name="norm_proj",
    )(x2, gains, w_stack)


def _retention_kernel(q_ref, k_ref, v_ref, g_ref, cos_ref, sin_ref, o_ref, state_ref, *, cb):
    @pl.when(pl.program_id(1) == 0)
    def _():
        state_ref[...] = jnp.zeros_like(state_ref)

    cosf = cos_ref[...]
    sinf = sin_ref[...]
    ri = lax.broadcasted_iota(jnp.int32, (cb, cb), 0)
    ci = lax.broadcasted_iota(jnp.int32, (cb, cb), 1)
    diff = (ri - ci).astype(F32)
    causal = ri >= ci
    idx = lax.broadcasted_iota(jnp.int32, (cb, 1), 0).astype(F32)
    for h in range(RET_HEADS):
        lg = math.log(1.0 - 2.0 ** (-5.0 - h))
        sl = slice(h * RET_DK, (h + 1) * RET_DK)
        q = q_ref[:, sl]
        k = k_ref[:, sl]
        v = v_ref[:, sl].astype(BF16)
        qr = q * cosf + pltpu.roll(q, RET_DK // 2, 1) * sinf
        kr = (k * cosf + pltpu.roll(k, RET_DK // 2, 1) * sinf) * (RET_DK ** -0.5)
        dmat = jnp.where(causal, jnp.exp(lg * jnp.where(causal, diff, 0.0)), 0.0)
        scores = _dot_nt(qr.astype(BF16), kr.astype(BF16)) * dmat
        o = _dot(scores.astype(BF16), v)
        q_dec = qr * jnp.exp(lg * (idx + 1.0))
        state = state_ref[h]
        o = o + _dot(q_dec.astype(BF16), state.astype(BF16))
        k_dec = kr * jnp.exp(lg * (cb - 1.0 - idx))
        state_ref[h] = math.exp(lg * cb) * state + _dot_tn(k_dec.astype(BF16), v)
        mu = jnp.mean(o, axis=-1, keepdims=True)
        oc = o - mu
        var = jnp.mean(oc * oc, axis=-1, keepdims=True)
        gate = g_ref[:, sl]
        o_ref[:, sl] = (oc * lax.rsqrt(var + 1e-5) * (gate * _sigmoid(gate))).astype(o_ref.dtype)


def _retention(main, cosf, sinf, B, S, cb=256):
    T = B * S
    nc = S // cb
    blk = lambda c: pl.BlockSpec((cb, RET_W), lambda b, i, c=c: (b * nc + i, c))
    tab = pl.BlockSpec((cb, RET_DK), lambda b, i: (i, 0))
    return pl.pallas_call(
        functools.partial(_retention_kernel, cb=cb),
        grid=(B, nc),
        in_specs=[blk(COL_RQ // RET_W), blk(COL_RK // RET_W), blk(COL_RV // RET_W),
                  blk(COL_RG // RET_W), tab, tab],
        out_specs=pl.BlockSpec((cb, RET_W), lambda b, i: (b * nc + i, 0)),
        out_shape=jax.ShapeDtypeStruct((T, RET_W), BF16),
        scratch_shapes=[pltpu.VMEM((RET_HEADS, RET_DK, RET_DV), F32)],
        compiler_params=pltpu.CompilerParams(
            dimension_semantics=("parallel", "arbitrary"), vmem_limit_bytes=VMEM_LIMIT),
        name="retention",
    )(main, main, main, main, cosf, sinf)


def _gelu_tanh(y):
    return 0.5 * y * (1.0 + jnp.tanh(math.sqrt(2.0 / math.pi) * (y + 0.044715 * (y * y * y))))


def _compress_kernel(r0_ref, r1_ref, pos_ref, w1_ref, w2_ref, o_ref):
    for g, r_ref in enumerate((r0_ref, r1_ref)):
        r = r_ref[0]
        n = r.shape[0]
        y_lo = _dot((r + pos_ref[0]).astype(BF16), w1_ref[0])
        y_hi = _dot((r + pos_ref[1]).astype(BF16), w1_ref[1])
        y = y_lo + pltpu.roll(y_hi, n - 1, 0)
        out = _dot(_gelu_tanh(y).astype(BF16), w2_ref[...])
        row = lax.broadcasted_iota(jnp.int32, out.shape, 0)
        o_ref[0, g] = jnp.where(row < n - 1, out, 0.0)


def _compress(cmp0, cmp1, pos, w1, w2, B, S):
    nr = S // CMP_STRIDE
    kw = CMP_STRIDE * LANES
    r0 = cmp0.reshape(B, nr, kw)
    r1 = cmp1.reshape(B, nr, kw)
    rspec = pl.BlockSpec((1, nr, kw), lambda b: (b, 0, 0))
    return pl.pallas_call(
        _compress_kernel,
        grid=(B,),
        in_specs=[rspec, rspec,
                  pl.BlockSpec((2, 1, kw), lambda b: (0, 0, 0)),
                  pl.BlockSpec((2, kw, LANES), lambda b: (0, 0, 0)),
                  pl.BlockSpec((LANES, LANES), lambda b: (0, 0))],
        out_specs=pl.BlockSpec((1, NSA_KV_GROUPS, nr, LANES), lambda b: (b, 0, 0, 0)),
        out_shape=jax.ShapeDtypeStruct((B, NSA_KV_GROUPS, nr, LANES), F32),
        compiler_params=pltpu.CompilerParams(
            dimension_semantics=("parallel",), vmem_limit_bytes=VMEM_LIMIT),
        name="nsa_compress",
    )(r0, r1, pos, w1, w2)


NSA_WBLK = LANES
NSA_VPAD = 16


def _tile_heads(x):
    return jnp.concatenate([x] * NSA_HPG, axis=1)


def _nsa_kernel(q_ref, cmp_ref, slc_ref, win_ref, gate_ref, expand_ref, o_ref,
                ks_ref, vst_ref, kw_ref, vwt_ref, m_ref, acc_ref, *, S, kc):
    QT = q_ref.shape[0]
    qb = pl.program_id(2)
    q0 = qb * QT
    n_slc = S // SLC_BLOCK
    n_cmp = S // CMP_STRIDE
    hd, H = NSA_HD, NSA_HPG

    @pl.when(qb == 0)
    def _():
        slab = slc_ref[...]
        ks_ref[:, :hd] = slab[:, :hd].astype(BF16)
        ks_ref[:, hd:] = expand_ref[...]
        ones_rows = lambda n: jnp.where(lax.broadcasted_iota(jnp.int32, (NSA_VPAD, n), 0) == 0, 1.0, 0.0)
        for c in range(S // kc):
            vt = slab[c * kc:(c + 1) * kc].T[hd:]
            vst_ref[c] = jnp.concatenate([vt, ones_rows(kc)], axis=0).astype(BF16)
        slab = win_ref[...]
        kw_ref[...] = slab[:, :hd].astype(BF16)
        for c in range(S // NSA_WBLK):
            vt = slab[c * NSA_WBLK:(c + 1) * NSA_WBLK].T[hd:]
            vwt_ref[c] = jnp.concatenate([vt, ones_rows(NSA_WBLK)], axis=0).astype(BF16)

    q_t = (q_ref[...] * (hd ** -0.5)).T
    q_f = jnp.concatenate([q_t[h * hd:(h + 1) * hd] for h in range(H)], axis=1)
    q_cat = q_f.astype(BF16)
    q_cat2 = (q_f * LOG2E).astype(BF16)
    t_row = q0 + lax.broadcasted_iota(jnp.int32, (1, QT), 1)

    cmp_t = cmp_ref[0, 0].T
    k_cmp = cmp_ref[0, 0, :, :hd].astype(BF16)
    cmp_end = lax.broadcasted_iota(jnp.int32, (n_cmp, 1), 0) * CMP_STRIDE + (CMP_BLOCK - 1)
    s_c = _dot(k_cmp, q_cat) + _tile_heads(jnp.where(cmp_end <= t_row, 0.0, NEG))
    wlen = WINDOW + QT
    w0 = pl.multiple_of(jnp.maximum(q0 - WINDOW, 0), NSA_WBLK)
    dist = t_row - (w0 + lax.broadcasted_iota(jnp.int32, (wlen, 1), 0))
    bias_w = jnp.where(dist >= 0, jnp.where(dist < WINDOW, 0.0, NEG), NEG)
    s_w = _dot(kw_ref[pl.ds(w0, wlen), :], q_cat2) + _tile_heads(bias_w)

    e = jnp.exp(s_c - jnp.max(s_c, axis=0, keepdims=True))
    any_c = _tile_heads(jnp.where(t_row >= CMP_BLOCK - 1, 1.0, 0.0))
    p_c = e * (any_c / jnp.maximum(jnp.sum(e, axis=0, keepdims=True), 1e-30))
    o_c = _dot(cmp_t[hd:].astype(BF16), p_c.astype(BF16))
    p_sum = p_c[:, :QT]
    for h in range(1, H):
        p_sum = p_sum + p_c[:, h * QT:(h + 1) * QT]

    sj = lax.broadcasted_iota(jnp.int32, (n_slc, n_cmp), 0) * SLC_BLOCK
    cn = lax.broadcasted_iota(jnp.int32, (n_slc, n_cmp), 1) * CMP_STRIDE
    overlap = jnp.where(cn <= sj + (SLC_BLOCK - 1),
                        jnp.where(cn + (CMP_BLOCK - 1) >= sj, 1.0, 0.0), 0.0).astype(BF16)
    imp = _dot_exact_lhs(overlap, p_sum)

    e = jnp.exp2((s_w - jnp.max(s_w, axis=0, keepdims=True)).astype(BF16))
    wb = w0 // NSA_WBLK
    o_w = _dot(vwt_ref[wb], e[:NSA_WBLK])
    for j in range(1, wlen // NSA_WBLK):
        o_w = o_w + _dot(vwt_ref[wb + j], e[j * NSA_WBLK:(j + 1) * NSA_WBLK])
    o_w = o_w[:hd] * (1.0 / jnp.maximum(o_w[hd:hd + 1], 1e-30))

    jblk = lax.broadcasted_iota(jnp.int32, (n_slc, QT), 0)
    bt = (q0 + lax.broadcasted_iota(jnp.int32, (n_slc, QT), 1)) // SLC_BLOCK
    back = bt - jblk
    bonus = jnp.where(jblk == 0, FORCE_BONUS,
                      jnp.where(back >= 0, jnp.where(back < FORCED_LOCAL, FORCE_BONUS, 0.0), 0.0))
    score = jnp.where(jblk <= bt, imp + bonus, NEG)
    SUB = 8
    groups = [score[r:r + SUB] for r in range(0, n_slc, SUB)]
    ranks = [jnp.zeros((SUB, QT), F32) for _ in groups]
    for i in range(n_slc):
        row = score[i:i + 1, :]
        for gi, grp in enumerate(groups):
            lo = gi * SUB
            if lo > i:
                ahead = row >= grp
            elif lo + SUB - 1 <= i:
                ahead = row > grp
            else:
                below = lo + lax.broadcasted_iota(jnp.int32, (SUB, QT), 0) > i
                ahead = jnp.where(below, jnp.where(row >= grp, 1.0, 0.0),
                                  jnp.where(row > grp, 1.0, 0.0)) > 0.5
            ranks[gi] = ranks[gi] + jnp.where(ahead, 1.0, 0.0)
    rank = jnp.concatenate(ranks, axis=0)
    sel_bias = jnp.where(rank < float(min(N_SELECT, n_slc)), 0.0, NEG).astype(BF16)

    ns = m_ref.shape[0]
    m_ref[...] = jnp.full_like(m_ref, M_FLOOR)
    acc_ref[...] = jnp.zeros_like(acc_ref)
    q_aug = jnp.concatenate([q_cat2, _tile_heads(sel_bias)], axis=0)
    gap = (q0 + lax.broadcasted_iota(jnp.int32, (kc, QT), 1)) - lax.broadcasted_iota(jnp.int32, (kc, QT), 0)

    def chunks(it, n_streams=ns, masked_from=ns):
        scores = []
        for j in range(n_streams):
            c = it * ns + j
            s = _dot(ks_ref[pl.ds(pl.multiple_of(c * kc, kc), kc), :], q_aug)
            if j >= masked_from:
                s = s + _tile_heads(jnp.where(gap >= c * kc, 0.0, NEG))
            scores.append(s)
        for j in range(n_streams):
            c = it * ns + j
            s = scores[j]
            m_old = m_ref[j]
            m_new = jnp.maximum(m_old, jnp.max(s, axis=0, keepdims=True))
            alpha = jnp.exp2(m_old - m_new)
            p = jnp.exp2((s - m_new).astype(BF16))
            acc_ref[j] = alpha * acc_ref[j] + _dot(vst_ref[c], p)
            m_ref[j] = m_new

    n_full = q0 // (ns * kc)
    lax.fori_loop(0, n_full, lambda it, carry: (chunks(it), carry)[1], 0)
    chunks(n_full, masked_from=0)
    m_all = m_ref[0]
    for j in range(1, ns):
        m_all = jnp.maximum(m_all, m_ref[j])
    acc = jnp.zeros(acc_ref.shape[1:], F32)
    for j in range(ns):
        acc = acc + jnp.exp2(m_ref[j] - m_all) * acc_ref[j]
    o_s = acc[:hd] * (1.0 / jnp.maximum(acc[hd:hd + 1], 1e-30))

    sig_t = _sigmoid(gate_ref[...]).T
    gate = lambda c: jnp.concatenate([sig_t[c * H + h:c * H + h + 1] for h in range(H)], axis=1)
    out_t = gate(0) * o_c + gate(1) * o_s + gate(2) * o_w
    out = jnp.concatenate([out_t[:, h * QT:(h + 1) * QT] for h in range(H)], axis=0)
    o_ref[...] = out.T.astype(o_ref.dtype)


def _nsa(main, kvcmp, B, S, kc=128, qt=256, ns=4):
    T = B * S
    nqb = S // qt
    G = NSA_KV_GROUPS
    assert S % (ns * kc) == 0 and (ns * kc) % qt == 0 and qt % kc == 0 and qt % NSA_WBLK == 0
    assert S >= WINDOW + qt
    n_slc = S // SLC_BLOCK
    expand = (jnp.arange(S, dtype=jnp.int32)[:, None] // SLC_BLOCK
              == jnp.arange(n_slc, dtype=jnp.int32)[None, :]).astype(BF16)
    return pl.pallas_call(
        functools.partial(_nsa_kernel, S=S, kc=kc),
        grid=(B, G, nqb),
        in_specs=[
            pl.BlockSpec((qt, NQ_W), lambda b, g, i: (b * nqb + i, COL_NQ // NQ_W + g)),
            pl.BlockSpec((1, 1, S // CMP_STRIDE, LANES), lambda b, g, i: (b, g, 0, 0)),
            pl.BlockSpec((S, LANES), lambda b, g, i: (b, COL_SLC // LANES + g)),
            pl.BlockSpec((S, LANES), lambda b, g, i: (b, COL_WIN // LANES + g)),
            pl.BlockSpec((qt, LANES), lambda b, g, i: (b * nqb + i, COL_GATE // LANES + g)),
            pl.BlockSpec((S, n_slc), lambda b, g, i: (0, 0)),
        ],
        out_specs=pl.BlockSpec((qt, NQ_W), lambda b, g, i: (b * nqb + i, g)),
        out_shape=jax.ShapeDtypeStruct((T, G * NQ_W), BF16),
        scratch_shapes=[
            pltpu.VMEM((S, NSA_HD + n_slc), BF16),
            pltpu.VMEM((S // kc, NSA_HD + NSA_VPAD, kc), BF16),
            pltpu.VMEM((S, NSA_HD), BF16),
            pltpu.VMEM((S // NSA_WBLK, NSA_HD + NSA_VPAD, NSA_WBLK), BF16),
            pltpu.VMEM((ns, 1, NSA_HPG * qt), F32),
            pltpu.VMEM((ns, NSA_HD + NSA_VPAD, NSA_HPG * qt), F32)],
        compiler_params=pltpu.CompilerParams(
            dimension_semantics=("parallel", "parallel", "arbitrary"), vmem_limit_bytes=VMEM_LIMIT),
        name="nsa_attention",
    )(main, kvcmp, main, main, main, expand)


HGRN_SUB = HGRN_CHUNK // 2
HGRN_SAFE_RANGE = 60.0


def _hgrn_kernel(q_ref, f_ref, i_ref, g_ref, lbl_ref, ng_ref, o_ref,
                 st_ref, oi_s, oc_s, q_s, k_s, b_s, v_s, *, rb, hp, layer_j):
    C, SB, DK = HGRN_CHUNK, HGRN_SUB, HGRN_DK
    nch = rb // C

    @pl.when(pl.program_id(2) == 0)
    def _():
        st_ref[...] = jnp.zeros_like(st_ref)

    logits = lbl_ref[...]
    e = jnp.exp(logits - jnp.max(logits, axis=0, keepdims=True))
    sm = e / jnp.sum(e, axis=0, keepdims=True)
    lb_all = jnp.maximum(jnp.sum(sm[:layer_j + 1], axis=0, keepdims=True) - sm[0:1], 0.0)

    ri = lax.broadcasted_iota(jnp.int32, (rb, rb), 0)
    ci = lax.broadcasted_iota(jnp.int32, (rb, rb), 1)
    lower = ((ri // C) == (ci // C)) & (ci <= ri)
    same_sub = (ri // SB) == (ci // SB)
    ltri = jnp.where(lower, 1.0, 0.0).astype(BF16)
    cat = lambda xs: jnp.concatenate(xs, axis=0).astype(BF16)

    heads = []
    span = jnp.zeros((1, DK), F32)
    for h in range(hp):
        hs = slice(h * DK, (h + 1) * DK)
        lb = lb_all[:, hs]
        fp = f_ref[:, hs]
        qp = q_ref[:, hs]
        v = i_ref[:, hs]
        log_sig = jnp.minimum(fp, 0.0) - _softplus_neg_abs(fp)
        c_term = jnp.log1p(-lb) + log_sig
        a_term = jnp.log(lb)
        log_f = jnp.maximum(a_term, c_term) + _softplus_neg_abs(a_term - c_term)
        k = (1.0 - lb) * _sigmoid(-fp)
        q = qp * _sigmoid(qp) * (DK ** -0.5)
        b = _dot_exact_lhs(ltri, log_f)
        for s0 in range(0, rb, SB):
            span = jnp.maximum(span, b[s0:s0 + 1] - b[s0 + SB - 1:s0 + SB])
        heads.append((q, k, b, v))

    attns = []
    for q, k, b, v in heads:
        qd, kd, qo, ko = [], [], [], []
        for s0 in range(0, rb, SB):
            bb = b[s0:s0 + SB]
            ref = b[s0 + SB // 2:s0 + SB // 2 + 1]
            qd.append(q[s0:s0 + SB] * jnp.exp(bb - ref))
            kd.append(k[s0:s0 + SB] * jnp.exp(ref - bb))
            if (s0 // SB) % 2 == 0:
                end = b[s0 + SB - 1:s0 + SB]
                ko.append(k[s0:s0 + SB] * jnp.exp(end - bb))
                qo.append(jnp.zeros((SB, DK), F32))
            else:
                end = b[s0 - 1:s0]
                qo.append(q[s0:s0 + SB] * jnp.exp(bb - end))
                ko.append(jnp.zeros((SB, DK), F32))
        a_diag = _dot_nt(cat(qd), cat(kd))
        a_off = _dot_nt(cat(qo), cat(ko))
        attns.append(jnp.where(lower, jnp.where(same_sub, a_diag, a_off), 0.0).astype(BF16))

    kvs = []
    for h, (q, k, b, v) in enumerate(heads):
        vb16 = v.astype(BF16)
        oi_s[h] = _dot(attns[h], vb16)
        kv = []
        for c in range(nch):
            sl = slice(c * C, (c + 1) * C)
            b_last = b[(c + 1) * C - 1:(c + 1) * C]
            kv.append(_dot_tn(vb16[sl], (k[sl] * jnp.exp(b_last - b[sl])).astype(BF16)))
        kvs.append(kv)

    for h, (q, k, b, v) in enumerate(heads):
        qb = (q * jnp.exp(b)).astype(BF16)
        st = st_ref[h]
        outs = []
        for c in range(nch):
            sl = slice(c * C, (c + 1) * C)
            outs.append(_dot_nt(qb[sl], st.astype(BF16)))
            st = jnp.exp(b[(c + 1) * C - 1:(c + 1) * C]) * st + kvs[h][c]
        st_ref[h] = st
        oc_s[h] = jnp.concatenate(outs, axis=0)

    @pl.when(jnp.max(span) > HGRN_SAFE_RANGE)
    def _():
        for h, (q, k, b, v) in enumerate(heads):
            q_s[...] = q
            k_s[...] = k
            b_s[...] = b
            v_s[...] = v

            def row(n, carry):
                c0 = pl.multiple_of((n // C) * C, C)
                qn = q_s[pl.ds(n, 1), :]
                bn = b_s[pl.ds(n, 1), :]
                kb = k_s[pl.ds(c0, C), :]
                bb = b_s[pl.ds(c0, C), :]
                vb = v_s[pl.ds(c0, C), :]
                rows = c0 + lax.broadcasted_iota(jnp.int32, (C, 1), 0)
                decay = jnp.exp(jnp.where(rows <= n, bn - bb, -jnp.inf))
                a = jnp.sum(qn * decay * kb, axis=-1, keepdims=True)
                oi_s[h, pl.ds(n, 1), :] = jnp.sum(a * vb, axis=0, keepdims=True)
                return carry
            lax.fori_loop(0, rb, row, 0)

    for h in range(hp):
        hs = slice(h * DK, (h + 1) * DK)
        o = oi_s[h] + oc_s[h]
        o_ref[:, hs] = (_rms(o, ng_ref[...]) * _sigmoid(g_ref[:, hs])).astype(o_ref.dtype)


def _hgrn(proj, lb_logits, norm_g, layer_j, B, S, rb=256, hp=8):
    T = B * S
    nrb = S // rb
    H = HGRN_HEADS
    ng = H // hp
    n_layers = lb_logits.shape[0]
    w = hp * HGRN_DK
    blk = lambda c: pl.BlockSpec((rb, w), lambda b, h, r, c=c: (b * nrb + r, c * ng + h))
    vm = lambda *lead: pltpu.VMEM((*lead, rb, HGRN_DK), F32)
    return pl.pallas_call(
        functools.partial(_hgrn_kernel, rb=rb, hp=hp, layer_j=layer_j),
        grid=(B, ng, nrb),
        in_specs=[blk(0), blk(1), blk(2), blk(3),
                  pl.BlockSpec((n_layers, w), lambda b, h, r: (0, h)),
                  pl.BlockSpec((1, HGRN_DV), lambda b, h, r: (0, 0))],
        out_specs=pl.BlockSpec((rb, w), lambda b, h, r: (b * nrb + r, h)),
        out_shape=jax.ShapeDtypeStruct((T, H * HGRN_DV), BF16),
        scratch_shapes=[pltpu.VMEM((hp, HGRN_DV, HGRN_DK), F32), vm(hp), vm(hp), vm(), vm(), vm(), vm()],
        compiler_params=pltpu.CompilerParams(
            dimension_semantics=("parallel", "parallel", "arbitrary"), vmem_limit_bytes=VMEM_LIMIT),
        name="hgrn2",
    )(proj, proj, proj, proj, lb_logits, norm_g.reshape(1, HGRN_DV))


def _mix_ffn_kernel(*refs, n_mix, final):
    x_ref = refs[0]
    mix_refs = refs[1:1 + n_mix]
    wo_refs = refs[1 + n_mix:1 + 2 * n_mix]
    g_ref, w1_ref, w3_ref, w2_ref = refs[1 + 2 * n_mix:5 + 2 * n_mix]
    o_ref = refs[-1]
    x1 = x_ref[...]
    for m_ref, wo_ref in zip(mix_refs, wo_refs):
        x1 = x1 + _dot(m_ref[...], wo_ref[...])
    h = _rms(x1, g_ref[...]).astype(BF16)
    a = _dot(h, w1_ref[...])
    u = (a * _sigmoid(a) * _dot(h, w3_ref[...])).astype(BF16)
    y = x1 + _dot(u, w2_ref[...])
    if final:
        y = _rms(y, refs[-2][...])
    o_ref[...] = y


def _mix_ffn(x2, mixes, wo_stack, j, gains, layer, w1s, w3s, w2s, final_g=None, tm=512):
    T, D = x2.shape
    assert T % tm == 0 and sum(m.shape[1] for m in mixes) == wo_stack.shape[1]
    n_mix = len(mixes)
    in_specs = [pl.BlockSpec((tm, D), lambda i: (i, 0))]
    in_specs += [pl.BlockSpec((tm, m.shape[1]), lambda i: (i, 0)) for m in mixes]
    rows = mixes[0].shape[1]
    assert all(m.shape[1] == rows for m in mixes)
    in_specs += [_layer_spec(wo_stack, j, rows, r) for r in range(n_mix)]
    in_specs += [_layer_spec(gains, layer), _layer_spec(w1s, layer), _layer_spec(w3s, layer),
                 _layer_spec(w2s, layer)]
    args = [x2, *mixes, *([wo_stack] * n_mix), gains, w1s, w3s, w2s]
    if final_g is not None:
        in_specs.append(pl.BlockSpec((1, D), lambda i: (0, 0)))
        args.append(final_g.reshape(1, D))
    return pl.pallas_call(
        functools.partial(_mix_ffn_kernel, n_mix=n_mix, final=final_g is not None),
        grid=(T // tm,),
        in_specs=in_specs,
        out_specs=pl.BlockSpec((tm, D), lambda i: (i, 0)),
        out_shape=jax.ShapeDtypeStruct((T, D), F32),
        compiler_params=pltpu.CompilerParams(
            dimension_semantics=("parallel",), vmem_limit_bytes=VMEM_LIMIT),
        name="mix_ffn",
    )(*args)


def _even_w_in_layout(w):
    hd, G = NSA_HD, NSA_KV_GROUPS
    o_kc = 4 * RET_W + NSA_HEADS * hd
    kv = lambda idx, g: w[..., o_kc + idx * G * hd + g * hd:o_kc + idx * G * hd + (g + 1) * hd]
    o_ng = o_kc + 6 * G * hd
    cols = [w[..., :o_kc]]
    for idx_k, idx_v in ((2, 3), (4, 5)):
        for g in range(G):
            cols += [kv(idx_k, g), kv(idx_v, g)]
    for g in range(G):
        gate = [w[..., o_ng + c * NSA_HEADS + g * NSA_HPG:o_ng + c * NSA_HEADS + (g + 1) * NSA_HPG]
                for c in range(3)]
        cols += gate + [jnp.zeros((*w.shape[:-1], LANES - 3 * NSA_HPG), w.dtype)]
    for g in range(G):
        cols += [kv(0, g), kv(1, g)]
    out = jnp.concatenate(cols, axis=-1)
    assert out.shape[-1] == EVEN_N
    return out.astype(BF16)


def _compress_weights(pos_k, w1_k, w2_k, pos_v, w1_v, w2_v):
    hd = NSA_HD
    z = jnp.zeros((CMP_BLOCK, hd, hd), F32)
    w1k = w1_k.reshape(CMP_BLOCK, hd, hd)
    w1v = w1_v.reshape(CMP_BLOCK, hd, hd)
    top = jnp.concatenate([w1k, z], axis=2)
    bot = jnp.concatenate([z, w1v], axis=2)
    w1 = jnp.concatenate([top, bot], axis=1)
    w1 = w1.reshape(2, CMP_STRIDE * LANES, LANES).astype(BF16)
    pos = jnp.concatenate([pos_k, pos_v], axis=1).reshape(2, 1, CMP_STRIDE * LANES)
    zz = jnp.zeros((hd, hd), F32)
    w2 = jnp.concatenate([jnp.concatenate([w2_k, zz], axis=1),
                          jnp.concatenate([zz, w2_v], axis=1)], axis=0).astype(BF16)
    return pos, w1, w2


def _rotary_tables(S):
    half = RET_DK // 2
    inv = ROPE_BASE ** (-jnp.arange(half, dtype=F32) / half)
    ang = jnp.arange(S, dtype=F32)[:, None] * inv[None, :]
    cos, sin = jnp.cos(ang), jnp.sin(ang)
    return jnp.concatenate([cos, cos], axis=1), jnp.concatenate([-sin, sin], axis=1)


def kernel(x, norm_mix_g, norm_ffn_g, final_norm_g, even_w_in, even_w_out, cmp_pos_k, cmp_w1_k, cmp_w2_k, cmp_pos_v, cmp_w1_v, cmp_w2_v, odd_w_in, odd_w_out, hgrn_norm_g, hgrn_lb_logits, ffn_w1, ffn_w3, ffn_w2):
    B, S, D = x.shape
    x2 = x.reshape(B * S, D)
    cosf, sinf = _rotary_tables(S)
    w_in_even = _even_w_in_layout(even_w_in)
    w_in_odd = odd_w_in.astype(BF16)
    w_out_even = even_w_out.astype(BF16)
    w_out_odd = odd_w_out.astype(BF16)
    w1s, w3s, w2s = ffn_w1.astype(BF16), ffn_w3.astype(BF16), ffn_w2.astype(BF16)
    g_mix = norm_mix_g.reshape(DEPTH, 1, D)
    g_ffn = norm_ffn_g.reshape(DEPTH, 1, D)
    for layer in range(DEPTH):
        j = layer // 2
        if layer % 2 == 0:
            main, cmp0, cmp1 = _norm_proj(x2, g_mix, layer, w_in_even, j, (EVEN_MAIN, LANES, LANES))
            pos, cw1, cw2 = _compress_weights(cmp_pos_k[j], cmp_w1_k[j], cmp_w2_k[j],
                                              cmp_pos_v[j], cmp_w1_v[j], cmp_w2_v[j])
            kvcmp = _compress(cmp0, cmp1, pos, cw1, cw2, B, S)
            o_r = _retention(main, cosf, sinf, B, S)
            o_n = _nsa(main, kvcmp, B, S)
            mixes, wo_stack = (o_r, o_n), w_out_even
        else:
            (proj,) = _norm_proj(x2, g_mix, layer, w_in_odd, j, (4 * D,))
            o_h = _hgrn(proj, hgrn_lb_logits, hgrn_norm_g[j], j, B, S)
            mixes, wo_stack = (o_h,), w_out_odd
        x2 = _mix_ffn(x2, mixes, wo_stack, j, g_ffn, layer, w1s, w3s, w2s,
                      final_g=final_norm_g if layer == DEPTH - 1 else None)
    return x2.reshape(B, S, D)
```

```python
import functools
import math

import jax
import jax.numpy as jnp
from jax import lax
from jax.experimental import pallas as pl
from jax.experimental.pallas import tpu as pltpu

F32 = jnp.float32
BF16 = jnp.bfloat16

D_MODEL = 1024
DEPTH = 4
RET_HEADS = 4
RET_DK = 128
RET_DV = 128
ROPE_BASE = 10000.0
NSA_HEADS = 8
NSA_KV_GROUPS = 2
NSA_HPG = NSA_HEADS // NSA_KV_GROUPS
NSA_HD = 64
CMP_BLOCK = 32
CMP_STRIDE = 16
SLC_BLOCK = 64
N_SELECT = 8
FORCED_LOCAL = 2
FORCE_BONUS = 1.0e4
WINDOW = 512
Q_BLOCK = 128
HGRN_HEADS = 8
HGRN_DK = D_MODEL // HGRN_HEADS
HGRN_DV = D_MODEL // HGRN_HEADS
HGRN_CHUNK = 64
D_FF = -(-8 * D_MODEL // (3 * 256)) * 256
EPS = 1e-6
EVEN_MIX = RET_HEADS * RET_DV + NSA_HEADS * NSA_HD

LANES = 128
VMEM_LIMIT = 56 * 1024 * 1024
NEG = -1e30
M_FLOOR = -1e29
LOG2E = math.log2(math.e)

RET_W = RET_HEADS * RET_DK
COL_RQ, COL_RK, COL_RV, COL_RG = 0, RET_W, 2 * RET_W, 3 * RET_W
COL_NQ = 4 * RET_W
NQ_W = NSA_HPG * NSA_HD
COL_SLC = COL_NQ + NSA_KV_GROUPS * NQ_W
COL_WIN = COL_SLC + NSA_KV_GROUPS * LANES
COL_GATE = COL_WIN + NSA_KV_GROUPS * LANES
EVEN_MAIN = COL_GATE + NSA_KV_GROUPS * LANES
EVEN_N = EVEN_MAIN + NSA_KV_GROUPS * LANES


def _dot(a, b):
    return jnp.dot(a, b, preferred_element_type=F32)


def _dot_nt(a, b):
    return lax.dot_general(a, b, (((1,), (1,)), ((), ())), preferred_element_type=F32)


def _dot_tn(a, b):
    return lax.dot_general(a, b, (((0,), (0,)), ((), ())), preferred_element_type=F32)


def _split3(x):
    hi = x.astype(BF16)
    r = x - hi.astype(F32)
    mid = r.astype(BF16)
    lo = (r - mid.astype(F32)).astype(BF16)
    return hi, mid, lo


def _dot_exact_lhs(sel, x):
    hi, mid, lo = _split3(x)
    return _dot(sel, hi) + _dot(sel, mid) + _dot(sel, lo)


def _dot_exact_rhs(x, sel):
    hi, mid, lo = _split3(x)
    return _dot(hi, sel) + _dot(mid, sel) + _dot(lo, sel)


def _rms(x, g):
    return x * lax.rsqrt(jnp.mean(x * x, axis=-1, keepdims=True) + EPS) * g


def _sigmoid(x):
    return 1.0 / (1.0 + jnp.exp(-x))


def _softplus_neg_abs(x):
    return jnp.log2(1.0 + jnp.exp2(jnp.abs(x) * (-LOG2E))) * math.log(2.0)


def _norm_proj_kernel(x_ref, g_ref, w_ref, *o_refs):
    h = _rms(x_ref[...], g_ref[...]).astype(BF16)
    off = 0
    for o_ref in o_refs:
        wd = o_ref.shape[1]
        o_ref[...] = _dot(h, w_ref[:, off:off + wd])
        off += wd


def _layer_spec(stack, layer, rows=None, row_block=0):
    _, r, c = stack.shape
    rows = r if rows is None else rows
    return pl.BlockSpec((None, rows, c), lambda *_: (layer, row_block, 0), pipeline_mode=pl.Buffered(1))


def _norm_proj(x2, gains, layer, w_stack, j, widths, tm=512):
    T, D = x2.shape
    N = w_stack.shape[2]
    assert sum(widths) == N and T % tm == 0
    return pl.pallas_call(
        _norm_proj_kernel,
        grid=(T // tm,),
        in_specs=[pl.BlockSpec((tm, D), lambda i: (i, 0)), _layer_spec(gains, layer), _layer_spec(w_stack, j)],
        out_specs=[pl.BlockSpec((tm, wd), lambda i: (i, 0)) for wd in widths],
        out_shape=[jax.ShapeDtypeStruct((T, wd), F32) for wd in widths],
        compiler_params=pltpu.CompilerParams(
            dimension_semantics=("parallel",), vmem_limit_bytes=VMEM_LIMIT),
        name="norm_proj",
    )(x2, gains, w_stack)


def _retention_kernel(q_ref, k_ref, v_ref, g_ref, cos_ref, sin_ref, o_ref,
                      state_ref, dmat_ref, qdec_ref, kdec_ref, *, cb):
    log_gamma = [math.log(1.0 - 2.0 ** (-5.0 - h)) for h in range(RET_HEADS)]

    @pl.when(pl.program_id(1) == 0)
    def _():
        state_ref[...] = jnp.zeros_like(state_ref)
        ri = lax.broadcasted_iota(jnp.int32, (cb, cb), 0)
        ci = lax.broadcasted_iota(jnp.int32, (cb, cb), 1)
        diff = (ri - ci).astype(F32)
        causal = ri >= ci
        idx = lax.broadcasted_iota(jnp.int32, (cb, RET_DK), 0).astype(F32)
        for h, lg in enumerate(log_gamma):
            dmat_ref[h] = jnp.where(causal, jnp.exp(lg * jnp.where(causal, diff, 0.0)), 0.0)
            qdec_ref[h] = jnp.exp(lg * (idx + 1.0))
            kdec_ref[h] = jnp.exp(lg * (cb - 1.0 - idx))

    cosf = cos_ref[...]
    sinf = sin_ref[...]
    for h, lg in enumerate(log_gamma):
        sl = slice(h * RET_DK, (h + 1) * RET_DK)
        q = q_ref[:, sl]
        k = k_ref[:, sl]
        v = v_ref[:, sl].astype(BF16)
        qr = q * cosf + pltpu.roll(q, RET_DK // 2, 1) * sinf
        kr = (k * cosf + pltpu.roll(k, RET_DK // 2, 1) * sinf) * (RET_DK ** -0.5)
        scores = _dot_nt(qr.astype(BF16), kr.astype(BF16)) * dmat_ref[h]
        o = _dot(scores.astype(BF16), v)
        q_dec = qr * qdec_ref[h]
        state = state_ref[h]
        o = o + _dot(q_dec.astype(BF16), state.astype(BF16))
        k_dec = kr * kdec_ref[h]
        state_ref[h] = math.exp(lg * cb) * state + _dot_tn(k_dec.astype(BF16), v)
        mu = jnp.mean(o, axis=-1, keepdims=True)
        oc = o - mu
        var = jnp.mean(oc * oc, axis=-1, keepdims=True)
        gate = g_ref[:, sl]
        o_ref[:, sl] = (oc * lax.rsqrt(var + 1e-5) * (gate * _sigmoid(gate))).astype(o_ref.dtype)


def _retention(main, cosf, sinf, B, S, cb=256):
    T = B * S
    nc = S // cb
    blk = lambda c: pl.BlockSpec((cb, RET_W), lambda b, i, c=c: (b * nc + i, c))
    tab = pl.BlockSpec((cb, RET_DK), lambda b, i: (i, 0))
    return pl.pallas_call(
        functools.partial(_retention_kernel, cb=cb),
        grid=(B, nc),
        in_specs=[blk(COL_RQ // RET_W), blk(COL_RK // RET_W), blk(COL_RV // RET_W),
                  blk(COL_RG // RET_W), tab, tab],
        out_specs=pl.BlockSpec((cb, RET_W), lambda b, i: (b * nc + i, 0)),
        out_shape=jax.ShapeDtypeStruct((T, RET_W), BF16),
        scratch_shapes=[pltpu.VMEM((RET_HEADS, RET_DK, RET_DV), F32),
                        pltpu.VMEM((RET_HEADS, cb, cb), F32),
                        pltpu.VMEM((RET_HEADS, cb, RET_DK), F32),
                        pltpu.VMEM((RET_HEADS, cb, RET_DK), F32)],
        compiler_params=pltpu.CompilerParams(
            dimension_semantics=("parallel", "arbitrary"), vmem_limit_bytes=VMEM_LIMIT),
        name="retention",
    )(main, main, main, main, cosf, sinf)


def _gelu_tanh(y):
    return 0.5 * y * (1.0 + jnp.tanh(math.sqrt(2.0 / math.pi) * (y + 0.044715 * (y * y * y))))


def _compress_kernel(r0_ref, r1_ref, pos_ref, w1_ref, w2_ref, o_ref):
    for g, r_ref in enumerate((r0_ref, r1_ref)):
        r = r_ref[0]
        n = r.shape[0]
        y_lo = _dot((r + pos_ref[0]).astype(BF16), w1_ref[0])
        y_hi = _dot((r + pos_ref[1]).astype(BF16), w1_ref[1])
        y = y_lo + pltpu.roll(y_hi, n - 1, 0)
        out = _dot(_gelu_tanh(y).astype(BF16), w2_ref[...])
        row = lax.broadcasted_iota(jnp.int32, out.shape, 0)
        o_ref[0, g] = jnp.where(row < n - 1, out, 0.0)


def _compress(cmp0, cmp1, pos, w1, w2, B, S):
    nr = S // CMP_STRIDE
    kw = CMP_STRIDE * LANES
    r0 = cmp0.reshape(B, nr, kw)
    r1 = cmp1.reshape(B, nr, kw)
    rspec = pl.BlockSpec((1, nr, kw), lambda b: (b, 0, 0))
    return pl.pallas_call(
        _compress_kernel,
        grid=(B,),
        in_specs=[rspec, rspec,
                  pl.BlockSpec((2, 1, kw), lambda b: (0, 0, 0)),
                  pl.BlockSpec((2, kw, LANES), lambda b: (0, 0, 0)),
                  pl.BlockSpec((LANES, LANES), lambda b: (0, 0))],
        out_specs=pl.BlockSpec((1, NSA_KV_GROUPS, nr, LANES), lambda b: (b, 0, 0, 0)),
        out_shape=jax.ShapeDtypeStruct((B, NSA_KV_GROUPS, nr, LANES), F32),
        compiler_params=pltpu.CompilerParams(
            dimension_semantics=("parallel",), vmem_limit_bytes=VMEM_LIMIT),
        name="nsa_compress",
    )(r0, r1, pos, w1, w2)


NSA_WBLK = LANES
NSA_VPAD = 16


def _tile_heads(x):
    return jnp.concatenate([x] * NSA_HPG, axis=1)


def _nsa_kernel(q_ref, cmp_ref, slc_ref, win_ref, gate_ref, expand_ref, o_ref,
                 ks_ref, vst_ref, kw_ref, vwt_ref, m_ref, acc_ref, *, S, kc):
    QT = q_ref.shape[0]
    G, hd, H = NSA_KV_GROUPS, NSA_HD, NSA_HPG
    qb = pl.program_id(1)
    q0 = qb * QT
    n_slc = S // SLC_BLOCK
    n_cmp = S // CMP_STRIDE
    ns = m_ref.shape[1]
    groups = range(G)

    @pl.when(qb == 0)
    def _():
        ones_rows = lambda n: jnp.where(lax.broadcasted_iota(jnp.int32, (NSA_VPAD, n), 0) == 0, 1.0, 0.0)
        for g in groups:
            slab = slc_ref[:, g * LANES:(g + 1) * LANES]
            ks_ref[g, :, :hd] = slab[:, :hd].astype(BF16)
            ks_ref[g, :, hd:] = expand_ref[...]
            for c in range(S // kc):
                vt = slab[c * kc:(c + 1) * kc].T[hd:]
                vst_ref[g, c] = jnp.concatenate([vt, ones_rows(kc)], axis=0).astype(BF16)
            slab = win_ref[:, g * LANES:(g + 1) * LANES]
            kw_ref[g] = slab[:, :hd].astype(BF16)
            for c in range(S // NSA_WBLK):
                vt = slab[c * NSA_WBLK:(c + 1) * NSA_WBLK].T[hd:]
                vwt_ref[g, c] = jnp.concatenate([vt, ones_rows(NSA_WBLK)], axis=0).astype(BF16)

    t_row = q0 + lax.broadcasted_iota(jnp.int32, (1, QT), 1)
    cmp_end = lax.broadcasted_iota(jnp.int32, (n_cmp, 1), 0) * CMP_STRIDE + (CMP_BLOCK - 1)
    bias_c = _tile_heads(jnp.where(cmp_end <= t_row, 0.0, NEG))
    any_c = _tile_heads(jnp.where(t_row >= CMP_BLOCK - 1, 1.0, 0.0))
    wlen = WINDOW + QT
    w0 = pl.multiple_of(jnp.maximum(q0 - WINDOW, 0), NSA_WBLK)
    wb = w0 // NSA_WBLK
    dist = t_row - (w0 + lax.broadcasted_iota(jnp.int32, (wlen, 1), 0))
    bias_w = _tile_heads(jnp.where(dist >= 0, jnp.where(dist < WINDOW, 0.0, NEG), NEG))

    q_cat, q_cat2 = [], []
    for g in groups:
        q_t = (q_ref[:, g * NQ_W:(g + 1) * NQ_W] * (hd ** -0.5)).T
        q_f = jnp.concatenate([q_t[h * hd:(h + 1) * hd] for h in range(H)], axis=1)
        q_cat.append(q_f.astype(BF16))
        q_cat2.append((q_f * LOG2E).astype(BF16))
    s_c = [_dot(cmp_ref[0, g, :, :hd].astype(BF16), q_cat[g]) + bias_c for g in groups]
    s_w = [_dot(kw_ref[g, pl.ds(w0, wlen), :], q_cat2[g]) + bias_w for g in groups]

    sj = lax.broadcasted_iota(jnp.int32, (n_slc, n_cmp), 0) * SLC_BLOCK
    cn = lax.broadcasted_iota(jnp.int32, (n_slc, n_cmp), 1) * CMP_STRIDE
    overlap = jnp.where(cn <= sj + (SLC_BLOCK - 1),
                        jnp.where(cn + (CMP_BLOCK - 1) >= sj, 1.0, 0.0), 0.0).astype(BF16)
    o_c, imp = [], []
    for g in groups:
        e = jnp.exp(s_c[g] - jnp.max(s_c[g], axis=0, keepdims=True))
        p_c = e * (any_c / jnp.maximum(jnp.sum(e, axis=0, keepdims=True), 1e-30))
        o_c.append(_dot(cmp_ref[0, g].T[hd:].astype(BF16), p_c.astype(BF16)))
        p_sum = p_c[:, :QT]
        for h in range(1, H):
            p_sum = p_sum + p_c[:, h * QT:(h + 1) * QT]
        imp.append(_dot_exact_lhs(overlap, p_sum))

    o_w = []
    for g in groups:
        e = jnp.exp2((s_w[g] - jnp.max(s_w[g], axis=0, keepdims=True)).astype(BF16))
        ow = _dot(vwt_ref[g, wb], e[:NSA_WBLK])
        for j in range(1, wlen // NSA_WBLK):
            ow = ow + _dot(vwt_ref[g, wb + j], e[j * NSA_WBLK:(j + 1) * NSA_WBLK])
        o_w.append(ow[:hd] * (1.0 / jnp.maximum(ow[hd:hd + 1], 1e-30)))

    jblk = lax.broadcasted_iota(jnp.int32, (n_slc, QT), 0)
    bt = (q0 + lax.broadcasted_iota(jnp.int32, (n_slc, QT), 1)) // SLC_BLOCK
    back = bt - jblk
    bonus = jnp.where(jblk == 0, FORCE_BONUS,
                      jnp.where(back >= 0, jnp.where(back < FORCED_LOCAL, FORCE_BONUS, 0.0), 0.0))
    SUB = 8
    q_aug = []
    for g in groups:
        score = jnp.where(jblk <= bt, imp[g] + bonus, NEG)
        rows = [score[r:r + SUB] for r in range(0, n_slc, SUB)]
        ranks = [jnp.zeros((SUB, QT), F32) for _ in rows]
        for i in range(n_slc):
            row = score[i:i + 1, :]
            for gi, grp in enumerate(rows):
                lo = gi * SUB
                if lo > i:
                    ahead = row >= grp
                elif lo + SUB - 1 <= i:
                    ahead = row > grp
                else:
                    below = lo + lax.broadcasted_iota(jnp.int32, (SUB, QT), 0) > i
                    ahead = jnp.where(below, jnp.where(row >= grp, 1.0, 0.0),
                                      jnp.where(row > grp, 1.0, 0.0)) > 0.5
                ranks[gi] = ranks[gi] + jnp.where(ahead, 1.0, 0.0)
        rank = jnp.concatenate(ranks, axis=0)
        sel_bias = jnp.where(rank < float(min(N_SELECT, n_slc)), 0.0, NEG).astype(BF16)
        q_aug.append(jnp.concatenate([q_cat2[g], _tile_heads(sel_bias)], axis=0))

    m_ref[...] = jnp.full_like(m_ref, M_FLOOR)
    acc_ref[...] = jnp.zeros_like(acc_ref)
    gap = (q0 + lax.broadcasted_iota(jnp.int32, (kc, QT), 1)) - lax.broadcasted_iota(jnp.int32, (kc, QT), 0)

    def chunks(it, causal_mask):
        scores = {}
        for j in range(ns):
            c = it * ns + j
            mask = _tile_heads(jnp.where(gap >= c * kc, 0.0, NEG)) if causal_mask else None
            for g in groups:
                s = _dot(ks_ref[g, pl.ds(pl.multiple_of(c * kc, kc), kc), :], q_aug[g])
                scores[g, j] = s + mask if causal_mask else s
        for j in range(ns):
            c = it * ns + j
            for g in groups:
                s = scores[g, j]
                m_old = m_ref[g, j]
                m_new = jnp.maximum(m_old, jnp.max(s, axis=0, keepdims=True))
                alpha = jnp.exp2(m_old - m_new)
                p = jnp.exp2((s - m_new).astype(BF16))
                acc_ref[g, j] = alpha * acc_ref[g, j] + _dot(vst_ref[g, c], p)
                m_ref[g, j] = m_new

    n_full = q0 // (ns * kc)
    lax.fori_loop(0, n_full, lambda it, carry: (chunks(it, False), carry)[1], 0)
    chunks(n_full, True)

    for g in groups:
        m_all = m_ref[g, 0]
        for j in range(1, ns):
            m_all = jnp.maximum(m_all, m_ref[g, j])
        acc = jnp.zeros(acc_ref.shape[2:], F32)
        for j in range(ns):
            acc = acc + jnp.exp2(m_ref[g, j] - m_all) * acc_ref[g, j]
        o_s = acc[:hd] * (1.0 / jnp.maximum(acc[hd:hd + 1], 1e-30))
        sig_t = _sigmoid(gate_ref[:, g * LANES:(g + 1) * LANES]).T
        gate = lambda c: jnp.concatenate([sig_t[c * H + h:c * H + h + 1] for h in range(H)], axis=1)
        out_t = gate(0) * o_c[g] + gate(1) * o_s + gate(2) * o_w[g]
        out = jnp.concatenate([out_t[:, h * QT:(h + 1) * QT] for h in range(H)], axis=0)
        o_ref[:, g * NQ_W:(g + 1) * NQ_W] = out.T.astype(o_ref.dtype)


def _nsa(main, kvcmp, B, S, kc=128, qt=256, ns=4):
    T = B * S
    nqb = S // qt
    G = NSA_KV_GROUPS
    assert S % (ns * kc) == 0 and (ns * kc) % qt == 0 and qt % kc == 0 and qt % NSA_WBLK == 0
    assert S >= WINDOW + qt
    n_slc = S // SLC_BLOCK
    expand = (jnp.arange(S, dtype=jnp.int32)[:, None] // SLC_BLOCK
              == jnp.arange(n_slc, dtype=jnp.int32)[None, :]).astype(BF16)
    gw = G * LANES
    vrows = NSA_HD + NSA_VPAD
    return pl.pallas_call(
        functools.partial(_nsa_kernel, S=S, kc=kc),
        grid=(B, nqb),
        in_specs=[
            pl.BlockSpec((qt, G * NQ_W), lambda b, i: (b * nqb + i, COL_NQ // (G * NQ_W))),
            pl.BlockSpec((1, G, S // CMP_STRIDE, LANES), lambda b, i: (b, 0, 0, 0)),
            pl.BlockSpec((S, gw), lambda b, i: (b, COL_SLC // gw)),
            pl.BlockSpec((S, gw), lambda b, i: (b, COL_WIN // gw)),
            pl.BlockSpec((qt, gw), lambda b, i: (b * nqb + i, COL_GATE // gw)),
            pl.BlockSpec((S, n_slc), lambda b, i: (0, 0)),
        ],
        out_specs=pl.BlockSpec((qt, G * NQ_W), lambda b, i: (b * nqb + i, 0)),
        out_shape=jax.ShapeDtypeStruct((T, G * NQ_W), BF16),
        scratch_shapes=[
            pltpu.VMEM((G, S, NSA_HD + n_slc), BF16),
            pltpu.VMEM((G, S // kc, vrows, kc), BF16),
            pltpu.VMEM((G, S, NSA_HD), BF16),
            pltpu.VMEM((G, S // NSA_WBLK, vrows, NSA_WBLK), BF16),
            pltpu.VMEM((G, ns, 1, NSA_HPG * qt), F32),
            pltpu.VMEM((G, ns, vrows, NSA_HPG * qt), F32)],
        compiler_params=pltpu.CompilerParams(
            dimension_semantics=("parallel", "arbitrary"), vmem_limit_bytes=VMEM_LIMIT),
        name="nsa_attention",
    )(main, kvcmp, main, main, main, expand)


HGRN_SUB = HGRN_CHUNK // 2
HGRN_SAFE_RANGE = 60.0


def _hgrn_kernel(q_ref, f_ref, i_ref, g_ref, lbl_ref, ng_ref, o_ref,
                 st_ref, oi_s, oc_s, q_s, k_s, b_s, v_s, *, rb, hp, layer_j):
    C, SB, DK = HGRN_CHUNK, HGRN_SUB, HGRN_DK
    nch = rb // C

    @pl.when(pl.program_id(2) == 0)
    def _():
        st_ref[...] = jnp.zeros_like(st_ref)

    logits = lbl_ref[...]
    e = jnp.exp(logits - jnp.max(logits, axis=0, keepdims=True))
    sm = e / jnp.sum(e, axis=0, keepdims=True)
    lb_all = jnp.maximum(jnp.sum(sm[:layer_j + 1], axis=0, keepdims=True) - sm[0:1], 0.0)

    ri = lax.broadcasted_iota(jnp.int32, (rb, rb), 0)
    ci = lax.broadcasted_iota(jnp.int32, (rb, rb), 1)
    lower = ((ri // C) == (ci // C)) & (ci <= ri)
    same_sub = (ri // SB) == (ci // SB)
    ltri = jnp.where(lower, 1.0, 0.0).astype(BF16)
    cat = lambda xs: jnp.concatenate(xs, axis=0).astype(BF16)

    heads = []
    span = jnp.zeros((1, DK), F32)
    for h in range(hp):
        hs = slice(h * DK, (h + 1) * DK)
        lb = lb_all[:, hs]
        fp = f_ref[:, hs]
        qp = q_ref[:, hs]
        v = i_ref[:, hs]
        log_sig = jnp.minimum(fp, 0.0) - _softplus_neg_abs(fp)
        c_term = jnp.log1p(-lb) + log_sig
        a_term = jnp.log(lb)
        log_f = jnp.maximum(a_term, c_term) + _softplus_neg_abs(a_term - c_term)
        k = (1.0 - lb) * _sigmoid(-fp)
        q = qp * _sigmoid(qp) * (DK ** -0.5)
        b = _dot_exact_lhs(ltri, log_f)
        for s0 in range(0, rb, SB):
            span = jnp.maximum(span, b[s0:s0 + 1] - b[s0 + SB - 1:s0 + SB])
        heads.append((q, k, b, v))

    b2s = [b * LOG2E for _, _, b, _ in heads]
    attns = []
    for (q, k, _, v), b2 in zip(heads, b2s):
        qd, kd, qo, ko = [], [], [], []
        for s0 in range(0, rb, SB):
            bb = b2[s0:s0 + SB]
            ref = b2[s0 + SB // 2:s0 + SB // 2 + 1]
            qd.append(q[s0:s0 + SB] * jnp.exp2(bb - ref))
            kd.append(k[s0:s0 + SB] * jnp.exp2(ref - bb))
            if (s0 // SB) % 2 == 0:
                end = b2[s0 + SB - 1:s0 + SB]
                ko.append(k[s0:s0 + SB] * jnp.exp2(end - bb))
                qo.append(jnp.zeros((SB, DK), F32))
            else:
                end = b2[s0 - 1:s0]
                qo.append(q[s0:s0 + SB] * jnp.exp2(bb - end))
                ko.append(jnp.zeros((SB, DK), F32))
        a_diag = _dot_nt(cat(qd), cat(kd))
        a_off = _dot_nt(cat(qo), cat(ko))
        attns.append(jnp.where(lower, jnp.where(same_sub, a_diag, a_off), 0.0).astype(BF16))

    kvs = []
    for h, ((q, k, _, v), b2) in enumerate(zip(heads, b2s)):
        vb16 = v.astype(BF16)
        oi_s[h] = _dot(attns[h], vb16)
        kv = []
        for c in range(nch):
            sl = slice(c * C, (c + 1) * C)
            b_last = b2[(c + 1) * C - 1:(c + 1) * C]
            kv.append(_dot_tn(vb16[sl], (k[sl] * jnp.exp2(b_last - b2[sl])).astype(BF16)))
        kvs.append(kv)

    for h, ((q, k, _, v), b2) in enumerate(zip(heads, b2s)):
        qb = (q * jnp.exp2(b2)).astype(BF16)
        st = st_ref[h]
        outs = []
        for c in range(nch):
            sl = slice(c * C, (c + 1) * C)
            outs.append(_dot_nt(qb[sl], st.astype(BF16)))
            st = jnp.exp2(b2[(c + 1) * C - 1:(c + 1) * C]) * st + kvs[h][c]
        st_ref[h] = st
        oc_s[h] = jnp.concatenate(outs, axis=0)

    @pl.when(jnp.max(span) > HGRN_SAFE_RANGE)
    def _():
        for h, (q, k, b, v) in enumerate(heads):
            q_s[...] = q
            k_s[...] = k
            b_s[...] = b
            v_s[...] = v

            def row(n, carry):
                c0 = pl.multiple_of((n // C) * C, C)
                qn = q_s[pl.ds(n, 1), :]
                bn = b_s[pl.ds(n, 1), :]
                kb = k_s[pl.ds(c0, C), :]
                bb = b_s[pl.ds(c0, C), :]
                vb = v_s[pl.ds(c0, C), :]
                rows = c0 + lax.broadcasted_iota(jnp.int32, (C, 1), 0)
                decay = jnp.exp(jnp.where(rows <= n, bn - bb, -jnp.inf))
                a = jnp.sum(qn * decay * kb, axis=-1, keepdims=True)
                oi_s[h, pl.ds(n, 1), :] = jnp.sum(a * vb, axis=0, keepdims=True)
                return carry
            lax.fori_loop(0, rb, row, 0)

    for h in range(hp):
        hs = slice(h * DK, (h + 1) * DK)
        o = oi_s[h] + oc_s[h]
        o_ref[:, hs] = (_rms(o, ng_ref[...]) * _sigmoid(g_ref[:, hs])).astype(o_ref.dtype)


def _hgrn(proj, lb_logits, norm_g, layer_j, B, S, rb=256, hp=8):
    T = B * S
    nrb = S // rb
    H = HGRN_HEADS
    ng = H // hp
    n_layers = lb_logits.shape[0]
    w = hp * HGRN_DK
    blk = lambda c: pl.BlockSpec((rb, w), lambda b, h, r, c=c: (b * nrb + r, c * ng + h))
    vm = lambda *lead: pltpu.VMEM((*lead, rb, HGRN_DK), F32)
    return pl.pallas_call(
        functools.partial(_hgrn_kernel, rb=rb, hp=hp, layer_j=layer_j),
        grid=(B, ng, nrb),
        in_specs=[blk(0), blk(1), blk(2), blk(3),
                  pl.BlockSpec((n_layers, w), lambda b, h, r: (0, h)),
                  pl.BlockSpec((1, HGRN_DV), lambda b, h, r: (0, 0))],
        out_specs=pl.BlockSpec((rb, w), lambda b, h, r: (b * nrb + r, h)),
        out_shape=jax.ShapeDtypeStruct((T, H * HGRN_DV), BF16),
        scratch_shapes=[pltpu.VMEM((hp, HGRN_DV, HGRN_DK), F32), vm(hp), vm(hp), vm(), vm(), vm(), vm()],
        compiler_params=pltpu.CompilerParams(
            dimension_semantics=("parallel", "parallel", "arbitrary"), vmem_limit_bytes=VMEM_LIMIT),
        name="hgrn2",
    )(proj, proj, proj, proj, lb_logits, norm_g.reshape(1, HGRN_DV))


def _mix_ffn_kernel(*refs, n_mix, final):
    x_ref = refs[0]
    mix_refs = refs[1:1 + n_mix]
    wo_refs = refs[1 + n_mix:1 + 2 * n_mix]
    g_ref, w1_ref, w3_ref, w2_ref = refs[1 + 2 * n_mix:5 + 2 * n_mix]
    o_ref = refs[-1]
    x1 = x_ref[...]
    for m_ref, wo_ref in zip(mix_refs, wo_refs):
        x1 = x1 + _dot(m_ref[...], wo_ref[...])
    h = _rms(x1, g_ref[...]).astype(BF16)
    a = _dot(h, w1_ref[...])
    u = (a * _sigmoid(a) * _dot(h, w3_ref[...])).astype(BF16)
    y = x1 + _dot(u, w2_ref[...])
    if final:
        y = _rms(y, refs[-2][...])
    o_ref[...] = y


def _mix_ffn(x2, mixes, wo_stack, j, gains, layer, w1s, w3s, w2s, final_g=None, tm=512):
    T, D = x2.shape
    assert T % tm == 0 and sum(m.shape[1] for m in mixes) == wo_stack.shape[1]
    n_mix = len(mixes)
    in_specs = [pl.BlockSpec((tm, D), lambda i: (i, 0))]
    in_specs += [pl.BlockSpec((tm, m.shape[1]), lambda i: (i, 0)) for m in mixes]
    rows = mixes[0].shape[1]
    assert all(m.shape[1] == rows for m in mixes)
    in_specs += [_layer_spec(wo_stack, j, rows, r) for r in range(n_mix)]
    in_specs += [_layer_spec(gains, layer), _layer_spec(w1s, layer), _layer_spec(w3s, layer),
                 _layer_spec(w2s, layer)]
    args = [x2, *mixes, *([wo_stack] * n_mix), gains, w1s, w3s, w2s]
    if final_g is not None:
        in_specs.append(pl.BlockSpec((1, D), lambda i: (0, 0)))
        args.append(final_g.reshape(1, D))
    return pl.pallas_call(
        functools.partial(_mix_ffn_kernel, n_mix=n_mix, final=final_g is not None),
        grid=(T // tm,),
        in_specs=in_specs,
        out_specs=pl.BlockSpec((tm, D), lambda i: (i, 0)),
        out_shape=jax.ShapeDtypeStruct((T, D), F32),
        compiler_params=pltpu.CompilerParams(
            dimension_semantics=("parallel",), vmem_limit_bytes=VMEM_LIMIT),
        name="mix_ffn",
    )(*args)


def _even_w_in_layout(w):
    hd, G = NSA_HD, NSA_KV_GROUPS
    o_kc = 4 * RET_W + NSA_HEADS * hd
    kv = lambda idx, g: w[..., o_kc + idx * G * hd + g * hd:o_kc + idx * G * hd + (g + 1) * hd]
    o_ng = o_kc + 6 * G * hd
    cols = [w[..., :o_kc]]
    for idx_k, idx_v in ((2, 3), (4, 5)):
        for g in range(G):
            cols += [kv(idx_k, g), kv(idx_v, g)]
    for g in range(G):
        gate = [w[..., o_ng + c * NSA_HEADS + g * NSA_HPG:o_ng + c * NSA_HEADS + (g + 1) * NSA_HPG]
                for c in range(3)]
        cols += gate + [jnp.zeros((*w.shape[:-1], LANES - 3 * NSA_HPG), w.dtype)]
    for g in range(G):
        cols += [kv(0, g), kv(1, g)]
    out = jnp.concatenate(cols, axis=-1)
    assert out.shape[-1] == EVEN_N
    return out.astype(BF16)


def _compress_weights(pos_k, w1_k, w2_k, pos_v, w1_v, w2_v):
    hd = NSA_HD
    z = jnp.zeros((CMP_BLOCK, hd, hd), F32)
    w1k = w1_k.reshape(CMP_BLOCK, hd, hd)
    w1v = w1_v.reshape(CMP_BLOCK, hd, hd)
    top = jnp.concatenate([w1k, z], axis=2)
    bot = jnp.concatenate([z, w1v], axis=2)
    w1 = jnp.concatenate([top, bot], axis=1)
    w1 = w1.reshape(2, CMP_STRIDE * LANES, LANES).astype(BF16)
    pos = jnp.concatenate([pos_k, pos_v], axis=1).reshape(2, 1, CMP_STRIDE * LANES)
    zz = jnp.zeros((hd, hd), F32)
    w2 = jnp.concatenate([jnp.concatenate([w2_k, zz], axis=1),
                          jnp.concatenate([zz, w2_v], axis=1)], axis=0).astype(BF16)
    return pos, w1, w2


def _rotary_tables(S):
    half = RET_DK // 2
    inv = ROPE_BASE ** (-jnp.arange(half, dtype=F32) / half)
    ang = jnp.arange(S, dtype=F32)[:, None] * inv[None, :]
    cos, sin = jnp.cos(ang), jnp.sin(ang)
    return jnp.concatenate([cos, cos], axis=1), jnp.concatenate([-sin, sin], axis=1)


def kernel(x, norm_mix_g, norm_ffn_g, final_norm_g, even_w_in, even_w_out, cmp_pos_k, cmp_w1_k, cmp_w2_k, cmp_pos_v, cmp_w1_v, cmp_w2_v, odd_w_in, odd_w_out, hgrn_norm_g, hgrn_lb_logits, ffn_w1, ffn_w3, ffn_w2):
    B, S, D = x.shape
    x2 = x.reshape(B * S, D)
    cosf, sinf = _rotary_tables(S)
    w_in_even = _even_w_in_layout(even_w_in)
    w_in_odd = odd_w_in.astype(BF16)
    w_out_even = even_w_out.astype(BF16)
    w_out_odd = odd_w_out.astype(BF16)
    w1s, w3s, w2s = ffn_w1.astype(BF16), ffn_w3.astype(BF16), ffn_w2.astype(BF16)
    g_mix = norm_mix_g.reshape(DEPTH, 1, D)
    g_ffn = norm_ffn_g.reshape(DEPTH, 1, D)
    for layer in range(DEPTH):
        j = layer // 2
        if layer % 2 == 0:
            main, cmp0, cmp1 = _norm_proj(x2, g_mix, layer, w_in_even, j, (EVEN_MAIN, LANES, LANES))
            pos, cw1, cw2 = _compress_weights(cmp_pos_k[j], cmp_w1_k[j], cmp_w2_k[j],
                                              cmp_pos_v[j], cmp_w1_v[j], cmp_w2_v[j])
            kvcmp = _compress(cmp0, cmp1, pos, cw1, cw2, B, S)
            o_r = _retention(main, cosf, sinf, B, S)
            o_n = _nsa(main, kvcmp, B, S)
            mixes, wo_stack = (o_r, o_n), w_out_even
        else:
            (proj,) = _norm_proj(x2, g_mix, layer, w_in_odd, j, (4 * D,))
            o_h = _hgrn(proj, hgrn_lb_logits, hgrn_norm_g[j], j, B, S)
            mixes, wo_stack = (o_h,), w_out_odd
        x2 = _mix_ffn(x2, mixes, wo_stack, j, g_ffn, layer, w1s, w3s, w2s,
                      final_g=final_norm_g if layer == DEPTH - 1 else None)
    return x2.reshape(B, S, D)
```

```python
import functools
import math

import jax
import jax.numpy as jnp
from jax import lax
from jax.experimental import pallas as pl
from jax.experimental.pallas import tpu as pltpu

F32 = jnp.float32
BF16 = jnp.bfloat16

D_MODEL = 1024
DEPTH = 4
RET_HEADS = 4
RET_DK = 128
RET_DV = 128
ROPE_BASE = 10000.0
NSA_HEADS = 8
NSA_KV_GROUPS = 2
NSA_HPG = NSA_HEADS // NSA_KV_GROUPS
NSA_HD = 64
CMP_BLOCK = 32
CMP_STRIDE = 16
SLC_BLOCK = 64
N_SELECT = 8
FORCED_LOCAL = 2
FORCE_BONUS = 1.0e4
WINDOW = 512
Q_BLOCK = 128
HGRN_HEADS = 8
HGRN_DK = D_MODEL // HGRN_HEADS
HGRN_DV = D_MODEL // HGRN_HEADS
HGRN_CHUNK = 64
D_FF = -(-8 * D_MODEL // (3 * 256)) * 256
EPS = 1e-6
EVEN_MIX = RET_HEADS * RET_DV + NSA_HEADS * NSA_HD

LANES = 128
VMEM_LIMIT = 56 * 1024 * 1024
NEG = -1e30
M_FLOOR = -1e29
LOG2E = math.log2(math.e)

RET_W = RET_HEADS * RET_DK
COL_RQ, COL_RK, COL_RV, COL_RG = 0, RET_W, 2 * RET_W, 3 * RET_W
COL_NQ = 4 * RET_W
NQ_W = NSA_HPG * NSA_HD
COL_SLC = COL_NQ + NSA_KV_GROUPS * NQ_W
COL_WIN = COL_SLC + NSA_KV_GROUPS * LANES
COL_GATE = COL_WIN + NSA_KV_GROUPS * LANES
EVEN_MAIN = COL_GATE + NSA_KV_GROUPS * LANES
EVEN_N = EVEN_MAIN + NSA_KV_GROUPS * LANES


def _dot(a, b):
    return jnp.dot(a, b, preferred_element_type=F32)


def _dot_nt(a, b):
    return lax.dot_general(a, b, (((1,), (1,)), ((), ())), preferred_element_type=F32)


def _dot_tn(a, b):
    return lax.dot_general(a, b, (((0,), (0,)), ((), ())), preferred_element_type=F32)


def _split3(x):
    hi = x.astype(BF16)
    r = x - hi.astype(F32)
    mid = r.astype(BF16)
    lo = (r - mid.astype(F32)).astype(BF16)
    return hi, mid, lo


def _dot_exact_lhs(sel, x):
    hi, mid, lo = _split3(x)
    return _dot(sel, hi) + _dot(sel, mid) + _dot(sel, lo)


def _dot_exact_rhs(x, sel):
    hi, mid, lo = _split3(x)
    return _dot(hi, sel) + _dot(mid, sel) + _dot(lo, sel)


def _rms(x, g):
    return x * lax.rsqrt(jnp.mean(x * x, axis=-1, keepdims=True) + EPS) * g


def _sigmoid(x):
    return 1.0 / (1.0 + jnp.exp(-x))


def _softplus_neg_abs(x):
    return jnp.log2(1.0 + jnp.exp2(jnp.abs(x) * (-LOG2E))) * math.log(2.0)


def _norm_proj_kernel(x_ref, g_ref, w_ref, *o_refs):
    h = _rms(x_ref[...], g_ref[...]).astype(BF16)
    off = 0
    for o_ref in o_refs:
        wd = o_ref.shape[1]
        o_ref[...] = _dot(h, w_ref[:, off:off + wd])
        off += wd


def _layer_spec(stack, layer, rows=None, row_block=0):
    _, r, c = stack.shape
    rows = r if rows is None else rows
    return pl.BlockSpec((None, rows, c), lambda *_: (layer, row_block, 0), pipeline_mode=pl.Buffered(1))


def _norm_proj(x2, gains, layer, w_stack, j, widths, tm=512):
    T, D = x2.shape
    N = w_stack.shape[2]
    assert sum(widths) == N and T % tm == 0
    return pl.pallas_call(
        _norm_proj_kernel,
        grid=(T // tm,),
        in_specs=[pl.BlockSpec((tm, D), lambda i: (i, 0)), _layer_spec(gains, layer), _layer_spec(w_stack, j)],
        out_specs=[pl.BlockSpec((tm, wd), lambda i: (i, 0)) for wd in widths],
        out_shape=[jax.ShapeDtypeStruct((T, wd), F32) for wd in widths],
        compiler_params=pltpu.CompilerParams(
            dimension_semantics=("parallel",), vmem_limit_bytes=VMEM_LIMIT),
        name="norm_proj",
    )(x2, gains, w_stack)


def _retention_kernel(q_ref, k_ref, v_ref, g_ref, cos_ref, sin_ref, o_ref,
                      state_ref, dmat_ref, qdec_ref, kdec_ref, *, cb):
    log_gamma = [math.log(1.0 - 2.0 ** (-5.0 - h)) for h in range(RET_HEADS)]

    @pl.when(pl.program_id(1) == 0)
    def _():
        state_ref[...] = jnp.zeros_like(state_ref)
        ri = lax.broadcasted_iota(jnp.int32, (cb, cb), 0)
        ci = lax.broadcasted_iota(jnp.int32, (cb, cb), 1)
        diff = (ri - ci).astype(F32)
        causal = ri >= ci
        idx = lax.broadcasted_iota(jnp.int32, (cb, RET_DK), 0).astype(F32)
        for h, lg in enumerate(log_gamma):
            dmat_ref[h] = jnp.where(causal, jnp.exp(lg * jnp.where(causal, diff, 0.0)), 0.0)
            qdec_ref[h] = jnp.exp(lg * (idx + 1.0))
            kdec_ref[h] = jnp.exp(lg * (cb - 1.0 - idx))

    cosf = cos_ref[...]
    sinf = sin_ref[...]
    for h, lg in enumerate(log_gamma):
        sl = slice(h * RET_DK, (h + 1) * RET_DK)
        q = q_ref[:, sl]
        k = k_ref[:, sl]
        v = v_ref[:, sl].astype(BF16)
        qr = q * cosf + pltpu.roll(q, RET_DK // 2, 1) * sinf
        kr = (k * cosf + pltpu.roll(k, RET_DK // 2, 1) * sinf) * (RET_DK ** -0.5)
        scores = _dot_nt(qr.astype(BF16), kr.astype(BF16)) * dmat_ref[h]
        o = _dot(scores.astype(BF16), v)
        q_dec = qr * qdec_ref[h]
        state = state_ref[h]
        o = o + _dot(q_dec.astype(BF16), state.astype(BF16))
        k_dec = kr * kdec_ref[h]
        state_ref[h] = math.exp(lg * cb) * state + _dot_tn(k_dec.astype(BF16), v)
        mu = jnp.mean(o, axis=-1, keepdims=True)
        oc = o - mu
        var = jnp.mean(oc * oc, axis=-1, keepdims=True)
        gate = g_ref[:, sl]
        o_ref[:, sl] = (oc * lax.rsqrt(var + 1e-5) * (gate * _sigmoid(gate))).astype(o_ref.dtype)


def _retention(main, cosf, sinf, B, S, cb=256):
    T = B * S
    nc = S // cb
    blk = lambda c: pl.BlockSpec((cb, RET_W), lambda b, i, c=c: (b * nc + i, c))
    tab = pl.BlockSpec((cb, RET_DK), lambda b, i: (i, 0))
    return pl.pallas_call(
        functools.partial(_retention_kernel, cb=cb),
        grid=(B, nc),
        in_specs=[blk(COL_RQ // RET_W), blk(COL_RK // RET_W), blk(COL_RV // RET_W),
                  blk(COL_RG // RET_W), tab, tab],
        out_specs=pl.BlockSpec((cb, RET_W), lambda b, i: (b * nc + i, 0)),
        out_shape=jax.ShapeDtypeStruct((T, RET_W), BF16),
        scratch_shapes=[pltpu.VMEM((RET_HEADS, RET_DK, RET_DV), F32),
                        pltpu.VMEM((RET_HEADS, cb, cb), F32),
                        pltpu.VMEM((RET_HEADS, cb, RET_DK), F32),
                        pltpu.VMEM((RET_HEADS, cb, RET_DK), F32)],
        compiler_params=pltpu.CompilerParams(
            dimension_semantics=("parallel", "arbitrary"), vmem_limit_bytes=VMEM_LIMIT),
        name="retention",
    )(main, main, main, main, cosf, sinf)


def _gelu_tanh(y):
    return 0.5 * y * (1.0 + jnp.tanh(math.sqrt(2.0 / math.pi) * (y + 0.044715 * (y * y * y))))


def _compress_kernel(r0_ref, r1_ref, pos_ref, w1_ref, w2_ref, o_ref):
    for g, r_ref in enumerate((r0_ref, r1_ref)):
        r = r_ref[0]
        n = r.shape[0]
        y_lo = _dot((r + pos_ref[0]).astype(BF16), w1_ref[0])
        y_hi = _dot((r + pos_ref[1]).astype(BF16), w1_ref[1])
        y = y_lo + pltpu.roll(y_hi, n - 1, 0)
        out = _dot(_gelu_tanh(y).astype(BF16), w2_ref[...])
        row = lax.broadcasted_iota(jnp.int32, out.shape, 0)
        o_ref[0, g] = jnp.where(row < n - 1, out, 0.0)


def _compress(cmp0, cmp1, pos, w1, w2, B, S):
    nr = S // CMP_STRIDE
    kw = CMP_STRIDE * LANES
    r0 = cmp0.reshape(B, nr, kw)
    r1 = cmp1.reshape(B, nr, kw)
    rspec = pl.BlockSpec((1, nr, kw), lambda b: (b, 0, 0))
    return pl.pallas_call(
        _compress_kernel,
        grid=(B,),
        in_specs=[rspec, rspec,
                  pl.BlockSpec((2, 1, kw), lambda b: (0, 0, 0)),
                  pl.BlockSpec((2, kw, LANES), lambda b: (0, 0, 0)),
                  pl.BlockSpec((LANES, LANES), lambda b: (0, 0))],
        out_specs=pl.BlockSpec((1, NSA_KV_GROUPS, nr, LANES), lambda b: (b, 0, 0, 0)),
        out_shape=jax.ShapeDtypeStruct((B, NSA_KV_GROUPS, nr, LANES), F32),
        compiler_params=pltpu.CompilerParams(
            dimension_semantics=("parallel",), vmem_limit_bytes=VMEM_LIMIT),
        name="nsa_compress",
    )(r0, r1, pos, w1, w2)


NSA_WBLK = 2 * LANES
NSA_VPAD = 16


def _tile_heads(x):
    return jnp.concatenate([x] * NSA_HPG, axis=1)


def _nsa_kernel(q_ref, cmp_ref, slc_ref, win_ref, gate_ref, expand_ref, o_ref,
                 ks_ref, vst_ref, kw_ref, vwt_ref, m_ref, acc_ref, *, S, kc):
    QT = q_ref.shape[0]
    G, hd, H = NSA_KV_GROUPS, NSA_HD, NSA_HPG
    qb = pl.program_id(1)
    q0 = qb * QT
    n_slc = S // SLC_BLOCK
    n_cmp = S // CMP_STRIDE
    ns = m_ref.shape[1]
    groups = range(G)

    @pl.when(qb == 0)
    def _():
        ones_rows = lambda n: jnp.where(lax.broadcasted_iota(jnp.int32, (NSA_VPAD, n), 0) == 0, 1.0, 0.0)
        for g in groups:
            slab = slc_ref[:, g * LANES:(g + 1) * LANES]
            ks_ref[g, :, :hd] = slab[:, :hd].astype(BF16)
            ks_ref[g, :, hd:] = expand_ref[...]
            for c in range(S // kc):
                vt = slab[c * kc:(c + 1) * kc].T[hd:]
                vst_ref[g, c] = jnp.concatenate([vt, ones_rows(kc)], axis=0).astype(BF16)
            slab = win_ref[:, g * LANES:(g + 1) * LANES]
            kw_ref[g] = slab[:, :hd].astype(BF16)
            for c in range(S // NSA_WBLK):
                vt = slab[c * NSA_WBLK:(c + 1) * NSA_WBLK].T[hd:]
                vwt_ref[g, c] = jnp.concatenate([vt, ones_rows(NSA_WBLK)], axis=0).astype(BF16)

    t_row = q0 + lax.broadcasted_iota(jnp.int32, (1, QT), 1)
    cmp_end = lax.broadcasted_iota(jnp.int32, (n_cmp, 1), 0) * CMP_STRIDE + (CMP_BLOCK - 1)
    bias_c = _tile_heads(jnp.where(cmp_end <= t_row, 0.0, NEG))
    any_c = _tile_heads(jnp.where(t_row >= CMP_BLOCK - 1, 1.0, 0.0))
    wlen = WINDOW + QT
    w0 = pl.multiple_of(jnp.maximum(q0 - WINDOW, 0), NSA_WBLK)
    wb = w0 // NSA_WBLK
    dist = t_row - (w0 + lax.broadcasted_iota(jnp.int32, (wlen, 1), 0))
    bias_w = _tile_heads(jnp.where(dist >= 0, jnp.where(dist < WINDOW, 0.0, NEG), NEG))

    q_cat, q_cat2 = [], []
    for g in groups:
        q_t = (q_ref[:, g * NQ_W:(g + 1) * NQ_W] * (hd ** -0.5)).T
        q_f = jnp.concatenate([q_t[h * hd:(h + 1) * hd] for h in range(H)], axis=1)
        q_cat.append(q_f.astype(BF16))
        q_cat2.append((q_f * LOG2E).astype(BF16))
    s_c = [_dot(cmp_ref[0, g, :, :hd].astype(BF16), q_cat[g]) + bias_c for g in groups]
    s_w = [_dot(kw_ref[g, pl.ds(w0, wlen), :], q_cat2[g]) + bias_w for g in groups]

    sj = lax.broadcasted_iota(jnp.int32, (n_slc, n_cmp), 0) * SLC_BLOCK
    cn = lax.broadcasted_iota(jnp.int32, (n_slc, n_cmp), 1) * CMP_STRIDE
    overlap = jnp.where(cn <= sj + (SLC_BLOCK - 1),
                        jnp.where(cn + (CMP_BLOCK - 1) >= sj, 1.0, 0.0), 0.0).astype(BF16)
    o_c, imp = [], []
    for g in groups:
        e = jnp.exp(s_c[g] - jnp.max(s_c[g], axis=0, keepdims=True))
        p_c = e * (any_c / jnp.maximum(jnp.sum(e, axis=0, keepdims=True), 1e-30))
        o_c.append(_dot(cmp_ref[0, g].T[hd:].astype(BF16), p_c.astype(BF16)))
        p_sum = p_c[:, :QT]
        for h in range(1, H):
            p_sum = p_sum + p_c[:, h * QT:(h + 1) * QT]
        imp.append(_dot_exact_lhs(overlap, p_sum))

    o_w = []
    for g in groups:
        e = jnp.exp2((s_w[g] - jnp.max(s_w[g], axis=0, keepdims=True)).astype(BF16))
        ow = _dot(vwt_ref[g, wb], e[:NSA_WBLK])
        for j in range(1, wlen // NSA_WBLK):
            ow = ow + _dot(vwt_ref[g, wb + j], e[j * NSA_WBLK:(j + 1) * NSA_WBLK])
        o_w.append(ow[:hd] * (1.0 / jnp.maximum(ow[hd:hd + 1], 1e-30)))

    jblk = lax.broadcasted_iota(jnp.int32, (n_slc, QT), 0)
    bt = (q0 + lax.broadcasted_iota(jnp.int32, (n_slc, QT), 1)) // SLC_BLOCK
    back = bt - jblk
    bonus = jnp.where(jblk == 0, FORCE_BONUS,
                      jnp.where(back >= 0, jnp.where(back < FORCED_LOCAL, FORCE_BONUS, 0.0), 0.0))
    jf = jblk.astype(F32)
    q_aug = []
    scores = [jnp.where(jblk <= bt, imp[g] + bonus, NEG) for g in groups]
    sel_biases = [jnp.full((n_slc, QT), NEG, F32) for g in groups]
    for _ in range(min(N_SELECT, n_slc)):
        for g in groups:
            top = jnp.max(scores[g], axis=0, keepdims=True)
            first = jnp.min(jnp.where(scores[g] == top, jf, float(n_slc)), axis=0, keepdims=True)
            hit = jf == first
            sel_biases[g] = jnp.where(hit, 0.0, sel_biases[g])
            scores[g] = jnp.where(hit, -jnp.inf, scores[g])
    for g in groups:
        sel_bias = sel_biases[g].astype(BF16)
        q_aug.append(jnp.concatenate([q_cat2[g], _tile_heads(sel_bias)], axis=0))

    m_ref[...] = jnp.full_like(m_ref, M_FLOOR)
    acc_ref[...] = jnp.zeros_like(acc_ref)
    gap = (q0 + lax.broadcasted_iota(jnp.int32, (kc, QT), 1)) - lax.broadcasted_iota(jnp.int32, (kc, QT), 0)

    def chunks(it, causal_mask):
        scores = {}
        for j in range(ns):
            c = it * ns + j
            mask = _tile_heads(jnp.where(gap >= c * kc, 0.0, NEG)) if causal_mask else None
            for g in groups:
                s = _dot(ks_ref[g, pl.ds(pl.multiple_of(c * kc, kc), kc), :], q_aug[g])
                scores[g, j] = s + mask if causal_mask else s
        for j in range(ns):
            c = it * ns + j
            for g in groups:
                s = scores[g, j]
                m_old = m_ref[g, j]
                m_new = jnp.maximum(m_old, jnp.max(s, axis=0, keepdims=True))
                alpha = jnp.exp2(m_old - m_new)
                p = jnp.exp2((s - m_new).astype(BF16))
                acc_ref[g, j] = alpha * acc_ref[g, j] + _dot(vst_ref[g, c], p)
                m_ref[g, j] = m_new

    n_full = q0 // (ns * kc)
    lax.fori_loop(0, n_full, lambda it, carry: (chunks(it, False), carry)[1], 0)
    chunks(n_full, True)

    for g in groups:
        m_all = m_ref[g, 0]
        for j in range(1, ns):
            m_all = jnp.maximum(m_all, m_ref[g, j])
        acc = jnp.zeros(acc_ref.shape[2:], F32)
        for j in range(ns):
            acc = acc + jnp.exp2(m_ref[g, j] - m_all) * acc_ref[g, j]
        o_s = acc[:hd] * (1.0 / jnp.maximum(acc[hd:hd + 1], 1e-30))
        sig_t = _sigmoid(gate_ref[:, g * LANES:(g + 1) * LANES]).T
        gate = lambda c: jnp.concatenate([sig_t[c * H + h:c * H + h + 1] for h in range(H)], axis=1)
        out_t = gate(0) * o_c[g] + gate(1) * o_s + gate(2) * o_w[g]
        out = jnp.concatenate([out_t[:, h * QT:(h + 1) * QT] for h in range(H)], axis=0)
        o_ref[:, g * NQ_W:(g + 1) * NQ_W] = out.T.astype(o_ref.dtype)


def _nsa(main, kvcmp, B, S, kc=128, qt=256, ns=4):
    T = B * S
    nqb = S // qt
    G = NSA_KV_GROUPS
    assert S % (ns * kc) == 0 and (ns * kc) % qt == 0 and qt % kc == 0 and qt % NSA_WBLK == 0
    assert S >= WINDOW + qt
    n_slc = S // SLC_BLOCK
    expand = (jnp.arange(S, dtype=jnp.int32)[:, None] // SLC_BLOCK
              == jnp.arange(n_slc, dtype=jnp.int32)[None, :]).astype(BF16)
    gw = G * LANES
    vrows = NSA_HD + NSA_VPAD
    return pl.pallas_call(
        functools.partial(_nsa_kernel, S=S, kc=kc),
        grid=(B, nqb),
        in_specs=[
            pl.BlockSpec((qt, G * NQ_W), lambda b, i: (b * nqb + i, COL_NQ // (G * NQ_W))),
            pl.BlockSpec((1, G, S // CMP_STRIDE, LANES), lambda b, i: (b, 0, 0, 0)),
            pl.BlockSpec((S, gw), lambda b, i: (b, COL_SLC // gw)),
            pl.BlockSpec((S, gw), lambda b, i: (b, COL_WIN // gw)),
            pl.BlockSpec((qt, gw), lambda b, i: (b * nqb + i, COL_GATE // gw)),
            pl.BlockSpec((S, n_slc), lambda b, i: (0, 0)),
        ],
        out_specs=pl.BlockSpec((qt, G * NQ_W), lambda b, i: (b * nqb + i, 0)),
        out_shape=jax.ShapeDtypeStruct((T, G * NQ_W), BF16),
        scratch_shapes=[
            pltpu.VMEM((G, S, NSA_HD + n_slc), BF16),
            pltpu.VMEM((G, S // kc, vrows, kc), BF16),
            pltpu.VMEM((G, S, NSA_HD), BF16),
            pltpu.VMEM((G, S // NSA_WBLK, vrows, NSA_WBLK), BF16),
            pltpu.VMEM((G, ns, 1, NSA_HPG * qt), F32),
            pltpu.VMEM((G, ns, vrows, NSA_HPG * qt), F32)],
        compiler_params=pltpu.CompilerParams(
            dimension_semantics=("parallel", "arbitrary"), vmem_limit_bytes=VMEM_LIMIT),
        name="nsa_attention",
    )(main, kvcmp, main, main, main, expand)


HGRN_SUB = HGRN_CHUNK // 2
HGRN_SAFE_RANGE = 60.0


def _hgrn_kernel(q_ref, f_ref, i_ref, g_ref, lbl_ref, ng_ref, o_ref,
                 st_ref, oi_s, oc_s, q_s, k_s, b_s, v_s, *, rb, hp, layer_j):
    C, SB, DK = HGRN_CHUNK, HGRN_SUB, HGRN_DK
    nch = rb // C

    @pl.when(pl.program_id(2) == 0)
    def _():
        st_ref[...] = jnp.zeros_like(st_ref)

    logits = lbl_ref[...]
    e = jnp.exp(logits - jnp.max(logits, axis=0, keepdims=True))
    sm = e / jnp.sum(e, axis=0, keepdims=True)
    lb_all = jnp.maximum(jnp.sum(sm[:layer_j + 1], axis=0, keepdims=True) - sm[0:1], 0.0)

    ri = lax.broadcasted_iota(jnp.int32, (rb, rb), 0)
    ci = lax.broadcasted_iota(jnp.int32, (rb, rb), 1)
    lower = ((ri // C) == (ci // C)) & (ci <= ri)
    same_sub = (ri // SB) == (ci // SB)
    ltri = jnp.where(lower, 1.0, 0.0).astype(BF16)
    cat = lambda xs: jnp.concatenate(xs, axis=0).astype(BF16)

    heads = []
    span = jnp.zeros((1, DK), F32)
    for h in range(hp):
        hs = slice(h * DK, (h + 1) * DK)
        lb = lb_all[:, hs]
        fp = f_ref[:, hs]
        qp = q_ref[:, hs]
        v = i_ref[:, hs]
        log_sig = jnp.minimum(fp, 0.0) - _softplus_neg_abs(fp)
        c_term = jnp.log1p(-lb) + log_sig
        a_term = jnp.log(lb)
        log_f = jnp.maximum(a_term, c_term) + _softplus_neg_abs(a_term - c_term)
        k = (1.0 - lb) * _sigmoid(-fp)
        q = qp * _sigmoid(qp) * (DK ** -0.5)
        b = _dot_exact_lhs(ltri, log_f)
        for s0 in range(0, rb, SB):
            span = jnp.maximum(span, b[s0:s0 + 1] - b[s0 + SB - 1:s0 + SB])
        heads.append((q, k, b, v))

    b2s = [b * LOG2E for _, _, b, _ in heads]
    attns = []
    for (q, k, _, v), b2 in zip(heads, b2s):
        qd, kd, qo, ko = [], [], [], []
        for s0 in range(0, rb, SB):
            bb = b2[s0:s0 + SB]
            ref = b2[s0 + SB // 2:s0 + SB // 2 + 1]
            qd.append(q[s0:s0 + SB] * jnp.exp2(bb - ref))
            kd.append(k[s0:s0 + SB] * jnp.exp2(ref - bb))
            if (s0 // SB) % 2 == 0:
                end = b2[s0 + SB - 1:s0 + SB]
                ko.append(k[s0:s0 + SB] * jnp.exp2(end - bb))
                qo.append(jnp.zeros((SB, DK), F32))
            else:
                end = b2[s0 - 1:s0]
                qo.append(q[s0:s0 + SB] * jnp.exp2(bb - end))
                ko.append(jnp.zeros((SB, DK), F32))
        a_diag = _dot_nt(cat(qd), cat(kd))
        a_off = _dot_nt(cat(qo), cat(ko))
        attns.append(jnp.where(lower, jnp.where(same_sub, a_diag, a_off), 0.0).astype(BF16))

    kvs = []
    for h, ((q, k, _, v), b2) in enumerate(zip(heads, b2s)):
        vb16 = v.astype(BF16)
        oi_s[h] = _dot(attns[h], vb16)
        kv = []
        for c in range(nch):
            sl = slice(c * C, (c + 1) * C)
            b_last = b2[(c + 1) * C - 1:(c + 1) * C]
            kv.append(_dot_tn(vb16[sl], (k[sl] * jnp.exp2(b_last - b2[sl])).astype(BF16)))
        kvs.append(kv)

    for h, ((q, k, _, v), b2) in enumerate(zip(heads, b2s)):
        qb = (q * jnp.exp2(b2)).astype(BF16)
        st = st_ref[h]
        outs = []
        for c in range(nch):
            sl = slice(c * C, (c + 1) * C)
            outs.append(_dot_nt(qb[sl], st.astype(BF16)))
            st = jnp.exp2(b2[(c + 1) * C - 1:(c + 1) * C]) * st + kvs[h][c]
        st_ref[h] = st
        oc_s[h] = jnp.concatenate(outs, axis=0)

    @pl.when(jnp.max(span) > HGRN_SAFE_RANGE)
    def _():
        for h, (q, k, b, v) in enumerate(heads):
            q_s[...] = q
            k_s[...] = k
            b_s[...] = b
            v_s[...] = v

            def row(n, carry):
                c0 = pl.multiple_of((n // C) * C, C)
                qn = q_s[pl.ds(n, 1), :]
                bn = b_s[pl.ds(n, 1), :]
                kb = k_s[pl.ds(c0, C), :]
                bb = b_s[pl.ds(c0, C), :]
                vb = v_s[pl.ds(c0, C), :]
                rows = c0 + lax.broadcasted_iota(jnp.int32, (C, 1), 0)
                decay = jnp.exp(jnp.where(rows <= n, bn - bb, -jnp.inf))
                a = jnp.sum(qn * decay * kb, axis=-1, keepdims=True)
                oi_s[h, pl.ds(n, 1), :] = jnp.sum(a * vb, axis=0, keepdims=True)
                return carry
            lax.fori_loop(0, rb, row, 0)

    for h in range(hp):
        hs = slice(h * DK, (h + 1) * DK)
        o = oi_s[h] + oc_s[h]
        o_ref[:, hs] = (_rms(o, ng_ref[...]) * _sigmoid(g_ref[:, hs])).astype(o_ref.dtype)


def _hgrn(proj, lb_logits, norm_g, layer_j, B, S, rb=256, hp=8):
    T = B * S
    nrb = S // rb
    H = HGRN_HEADS
    ng = H // hp
    n_layers = lb_logits.shape[0]
    w = hp * HGRN_DK
    blk = lambda c: pl.BlockSpec((rb, w), lambda b, h, r, c=c: (b * nrb + r, c * ng + h))
    vm = lambda *lead: pltpu.VMEM((*lead, rb, HGRN_DK), F32)
    return pl.pallas_call(
        functools.partial(_hgrn_kernel, rb=rb, hp=hp, layer_j=layer_j),
        grid=(B, ng, nrb),
        in_specs=[blk(0), blk(1), blk(2), blk(3),
                  pl.BlockSpec((n_layers, w), lambda b, h, r: (0, h)),
                  pl.BlockSpec((1, HGRN_DV), lambda b, h, r: (0, 0))],
        out_specs=pl.BlockSpec((rb, w), lambda b, h, r: (b * nrb + r, h)),
        out_shape=jax.ShapeDtypeStruct((T, H * HGRN_DV), BF16),
        scratch_shapes=[pltpu.VMEM((hp, HGRN_DV, HGRN_DK), F32), vm(hp), vm(hp), vm(), vm(), vm(), vm()],
        compiler_params=pltpu.CompilerParams(
            dimension_semantics=("parallel", "parallel", "arbitrary"), vmem_limit_bytes=VMEM_LIMIT),
        name="hgrn2",
    )(proj, proj, proj, proj, lb_logits, norm_g.reshape(1, HGRN_DV))


def _mix_ffn_kernel(*refs, n_mix, final):
    x_ref = refs[0]
    mix_refs = refs[1:1 + n_mix]
    wo_refs = refs[1 + n_mix:1 + 2 * n_mix]
    g_ref, w1_ref, w3_ref, w2_ref = refs[1 + 2 * n_mix:5 + 2 * n_mix]
    o_ref = refs[-1]
    x1 = x_ref[...]
    for m_ref, wo_ref in zip(mix_refs, wo_refs):
        x1 = x1 + _dot(m_ref[...], wo_ref[...])
    h = _rms(x1, g_ref[...]).astype(BF16)
    a = _dot(h, w1_ref[...])
    u = (a * _sigmoid(a) * _dot(h, w3_ref[...])).astype(BF16)
    y = x1 + _dot(u, w2_ref[...])
    if final:
        y = _rms(y, refs[-2][...])
    o_ref[...] = y


def _mix_ffn(x2, mixes, wo_stack, j, gains, layer, w1s, w3s, w2s, final_g=None, tm=512):
    T, D = x2.shape
    assert T % tm == 0 and sum(m.shape[1] for m in mixes) == wo_stack.shape[1]
    n_mix = len(mixes)
    in_specs = [pl.BlockSpec((tm, D), lambda i: (i, 0))]
    in_specs += [pl.BlockSpec((tm, m.shape[1]), lambda i: (i, 0)) for m in mixes]
    rows = mixes[0].shape[1]
    assert all(m.shape[1] == rows for m in mixes)
    in_specs += [_layer_spec(wo_stack, j, rows, r) for r in range(n_mix)]
    in_specs += [_layer_spec(gains, layer), _layer_spec(w1s, layer), _layer_spec(w3s, layer),
                 _layer_spec(w2s, layer)]
    args = [x2, *mixes, *([wo_stack] * n_mix), gains, w1s, w3s, w2s]
    if final_g is not None:
        in_specs.append(pl.BlockSpec((1, D), lambda i: (0, 0)))
        args.append(final_g.reshape(1, D))
    return pl.pallas_call(
        functools.partial(_mix_ffn_kernel, n_mix=n_mix, final=final_g is not None),
        grid=(T // tm,),
        in_specs=in_specs,
        out_specs=pl.BlockSpec((tm, D), lambda i: (i, 0)),
        out_shape=jax.ShapeDtypeStruct((T, D), F32),
        compiler_params=pltpu.CompilerParams(
            dimension_semantics=("parallel",), vmem_limit_bytes=VMEM_LIMIT),
        name="mix_ffn",
    )(*args)


def _even_w_in_layout(w):
    hd, G = NSA_HD, NSA_KV_GROUPS
    o_kc = 4 * RET_W + NSA_HEADS * hd
    kv = lambda idx, g: w[..., o_kc + idx * G * hd + g * hd:o_kc + idx * G * hd + (g + 1) * hd]
    o_ng = o_kc + 6 * G * hd
    cols = [w[..., :o_kc]]
    for idx_k, idx_v in ((2, 3), (4, 5)):
        for g in range(G):
            cols += [kv(idx_k, g), kv(idx_v, g)]
    for g in range(G):
        gate = [w[..., o_ng + c * NSA_HEADS + g * NSA_HPG:o_ng + c * NSA_HEADS + (g + 1) * NSA_HPG]
                for c in range(3)]
        cols += gate + [jnp.zeros((*w.shape[:-1], LANES - 3 * NSA_HPG), w.dtype)]
    for g in range(G):
        cols += [kv(0, g), kv(1, g)]
    out = jnp.concatenate(cols, axis=-1)
    assert out.shape[-1] == EVEN_N
    return out.astype(BF16)


def _compress_weights(pos_k, w1_k, w2_k, pos_v, w1_v, w2_v):
    hd = NSA_HD
    z = jnp.zeros((CMP_BLOCK, hd, hd), F32)
    w1k = w1_k.reshape(CMP_BLOCK, hd, hd)
    w1v = w1_v.reshape(CMP_BLOCK, hd, hd)
    top = jnp.concatenate([w1k, z], axis=2)
    bot = jnp.concatenate([z, w1v], axis=2)
    w1 = jnp.concatenate([top, bot], axis=1)
    w1 = w1.reshape(2, CMP_STRIDE * LANES, LANES).astype(BF16)
    pos = jnp.concatenate([pos_k, pos_v], axis=1).reshape(2, 1, CMP_STRIDE * LANES)
    zz = jnp.zeros((hd, hd), F32)
    w2 = jnp.concatenate([jnp.concatenate([w2_k, zz], axis=1),
                          jnp.concatenate([zz, w2_v], axis=1)], axis=0).astype(BF16)
    return pos, w1, w2


def _rotary_tables(S):
    half = RET_DK // 2
    inv = ROPE_BASE ** (-jnp.arange(half, dtype=F32) / half)
    ang = jnp.arange(S, dtype=F32)[:, None] * inv[None, :]
    cos, sin = jnp.cos(ang), jnp.sin(ang)
    return jnp.concatenate([cos, cos], axis=1), jnp.concatenate([-sin, sin], axis=1)


def kernel(x, norm_mix_g, norm_ffn_g, final_norm_g, even_w_in, even_w_out, cmp_pos_k, cmp_w1_k, cmp_w2_k, cmp_pos_v, cmp_w1_v, cmp_w2_v, odd_w_in, odd_w_out, hgrn_norm_g, hgrn_lb_logits, ffn_w1, ffn_w3, ffn_w2):
    B, S, D = x.shape
    x2 = x.reshape(B * S, D)
    cosf, sinf = _rotary_tables(S)
    w_in_even = _even_w_in_layout(even_w_in)
    w_in_odd = odd_w_in.astype(BF16)
    w_out_even = even_w_out.astype(BF16)
    w_out_odd = odd_w_out.astype(BF16)
    w1s, w3s, w2s = ffn_w1.astype(BF16), ffn_w3.astype(BF16), ffn_w2.astype(BF16)
    g_mix = norm_mix_g.reshape(DEPTH, 1, D)
    g_ffn = norm_ffn_g.reshape(DEPTH, 1, D)
    for layer in range(DEPTH):
        j = layer // 2
        if layer % 2 == 0:
            main, cmp0, cmp1 = _norm_proj(x2, g_mix, layer, w_in_even, j, (EVEN_MAIN, LANES, LANES))
            pos, cw1, cw2 = _compress_weights(cmp_pos_k[j], cmp_w1_k[j], cmp_w2_k[j],
                                              cmp_pos_v[j], cmp_w1_v[j], cmp_w2_v[j])
            kvcmp = _compress(cmp0, cmp1, pos, cw1, cw2, B, S)
            o_r = _retention(main, cosf, sinf, B, S)
            o_n = _nsa(main, kvcmp, B, S)
            mixes, wo_stack = (o_r, o_n), w_out_even
        else:
            (proj,) = _norm_proj(x2, g_mix, layer, w_in_odd, j, (4 * D,))
            o_h = _hgrn(proj, hgrn_lb_logits, hgrn_norm_g[j], j, B, S)
            mixes, wo_stack = (o_h,), w_out_odd
        x2 = _mix_ffn(x2, mixes, wo_stack, j, g_ffn, layer, w1s, w3s, w2s,
                      final_g=final_norm_g if layer == DEPTH - 1 else None)
    return x2.reshape(B, S, D)
```

```python
import functools
import math

import jax
import jax.numpy as jnp
from jax import lax
from jax.experimental import pallas as pl
from jax.experimental.pallas import tpu as pltpu

F32 = jnp.float32
BF16 = jnp.bfloat16

D_MODEL = 1024
DEPTH = 4
RET_HEADS = 4
RET_DK = 128
RET_DV = 128
ROPE_BASE = 10000.0
NSA_HEADS = 8
NSA_KV_GROUPS = 2
NSA_HPG = NSA_HEADS // NSA_KV_GROUPS
NSA_HD = 64
CMP_BLOCK = 32
CMP_STRIDE = 16
SLC_BLOCK = 64
N_SELECT = 8
FORCED_LOCAL = 2
FORCE_BONUS = 1.0e4
WINDOW = 512
Q_BLOCK = 128
HGRN_HEADS = 8
HGRN_DK = D_MODEL // HGRN_HEADS
HGRN_DV = D_MODEL // HGRN_HEADS
HGRN_CHUNK = 64
D_FF = -(-8 * D_MODEL // (3 * 256)) * 256
EPS = 1e-6
EVEN_MIX = RET_HEADS * RET_DV + NSA_HEADS * NSA_HD

LANES = 128
VMEM_LIMIT = 56 * 1024 * 1024
NEG = -1e30
M_FLOOR = -1e29
LOG2E = math.log2(math.e)

RET_W = RET_HEADS * RET_DK
COL_RQ, COL_RK, COL_RV, COL_RG = 0, RET_W, 2 * RET_W, 3 * RET_W
COL_NQ = 4 * RET_W
NQ_W = NSA_HPG * NSA_HD
COL_SLC = COL_NQ + NSA_KV_GROUPS * NQ_W
COL_WIN = COL_SLC + NSA_KV_GROUPS * LANES
COL_GATE = COL_WIN + NSA_KV_GROUPS * LANES
EVEN_MAIN = COL_GATE + NSA_KV_GROUPS * LANES
EVEN_N = EVEN_MAIN + NSA_KV_GROUPS * LANES


def _dot(a, b):
    return jnp.dot(a, b, preferred_element_type=F32)


def _dot_nt(a, b):
    return lax.dot_general(a, b, (((1,), (1,)), ((), ())), preferred_element_type=F32)


def _dot_tn(a, b):
    return lax.dot_general(a, b, (((0,), (0,)), ((), ())), preferred_element_type=F32)


def _split3(x):
    hi = x.astype(BF16)
    r = x - hi.astype(F32)
    mid = r.astype(BF16)
    lo = (r - mid.astype(F32)).astype(BF16)
    return hi, mid, lo


def _dot_exact_lhs(sel, x):
    hi, mid, lo = _split3(x)
    return _dot(sel, hi) + _dot(sel, mid) + _dot(sel, lo)


def _dot_exact_rhs(x, sel):
    hi, mid, lo = _split3(x)
    return _dot(hi, sel) + _dot(mid, sel) + _dot(lo, sel)


def _rms(x, g):
    return x * lax.rsqrt(jnp.mean(x * x, axis=-1, keepdims=True) + EPS) * g


def _sigmoid(x):
    return 1.0 / (1.0 + jnp.exp(-x))


def _softplus_neg_abs(x):
    return jnp.log2(1.0 + jnp.exp2(jnp.abs(x) * (-LOG2E))) * math.log(2.0)


def _norm_proj_kernel(x_ref, g_ref, w_ref, *o_refs):
    h = _rms(x_ref[...], g_ref[...]).astype(BF16)
    off = 0
    for o_ref in o_refs:
        wd = o_ref.shape[1]
        o_ref[...] = _dot(h, w_ref[:, off:off + wd])
        off += wd


def _layer_spec(stack, layer, rows=None, row_block=0):
    _, r, c = stack.shape
    rows = r if rows is None else rows
    return pl.BlockSpec((None, rows, c), lambda *_: (layer, row_block, 0), pipeline_mode=pl.Buffered(1))


def _norm_proj(x2, gains, layer, w_stack, j, widths, tm=512):
    T, D = x2.shape
    N = w_stack.shape[2]
    assert sum(widths) == N and T % tm == 0
    return pl.pallas_call(
        _norm_proj_kernel,
        grid=(T // tm,),
        in_specs=[pl.BlockSpec((tm, D), lambda i: (i, 0)), _layer_spec(gains, layer), _layer_spec(w_stack, j)],
        out_specs=[pl.BlockSpec((tm, wd), lambda i: (i, 0)) for wd in widths],
        out_shape=[jax.ShapeDtypeStruct((T, wd), F32) for wd in widths],
        compiler_params=pltpu.CompilerParams(
            dimension_semantics=("parallel",), vmem_limit_bytes=VMEM_LIMIT),
        name="norm_proj",
    )(x2, gains, w_stack)


RET_LOG_GAMMA = [math.log(1.0 - 2.0 ** (-5.0 - h)) for h in range(RET_HEADS)]


def _retention_init(state_ref, dmat_ref, qdec_ref, kdec_ref):
    cb = dmat_ref.shape[1]
    state_ref[...] = jnp.zeros_like(state_ref)
    ri = lax.broadcasted_iota(jnp.int32, (cb, cb), 0)
    ci = lax.broadcasted_iota(jnp.int32, (cb, cb), 1)
    diff = (ri - ci).astype(F32)
    causal = ri >= ci
    idx = lax.broadcasted_iota(jnp.int32, (cb, RET_DK), 0).astype(F32)
    for h, lg in enumerate(RET_LOG_GAMMA):
        dmat_ref[h] = jnp.where(causal, jnp.exp(lg * jnp.where(causal, diff, 0.0)), 0.0)
        qdec_ref[h] = jnp.exp(lg * (idx + 1.0))
        kdec_ref[h] = jnp.exp(lg * (cb - 1.0 - idx))


def _retention_step(q_ref, k_ref, v_ref, g_ref, cos_ref, sin_ref, o_ref,
                    state_ref, dmat_ref, qdec_ref, kdec_ref):
    cb = dmat_ref.shape[1]
    cosf = cos_ref[...]
    sinf = sin_ref[...]
    for h, lg in enumerate(RET_LOG_GAMMA):
        sl = slice(h * RET_DK, (h + 1) * RET_DK)
        q = q_ref[:, sl]
        k = k_ref[:, sl]
        v = v_ref[:, sl].astype(BF16)
        qr = q * cosf + pltpu.roll(q, RET_DK // 2, 1) * sinf
        kr = (k * cosf + pltpu.roll(k, RET_DK // 2, 1) * sinf) * (RET_DK ** -0.5)
        scores = _dot_nt(qr.astype(BF16), kr.astype(BF16)) * dmat_ref[h]
        o = _dot(scores.astype(BF16), v)
        q_dec = qr * qdec_ref[h]
        state = state_ref[h]
        o = o + _dot(q_dec.astype(BF16), state.astype(BF16))
        k_dec = kr * kdec_ref[h]
        state_ref[h] = math.exp(lg * cb) * state + _dot_tn(k_dec.astype(BF16), v)
        mu = jnp.mean(o, axis=-1, keepdims=True)
        oc = o - mu
        var = jnp.mean(oc * oc, axis=-1, keepdims=True)
        gate = g_ref[:, sl]
        o_ref[:, sl] = (oc * lax.rsqrt(var + 1e-5) * (gate * _sigmoid(gate))).astype(o_ref.dtype)


def _gelu_tanh(y):
    return 0.5 * y * (1.0 + jnp.tanh(math.sqrt(2.0 / math.pi) * (y + 0.044715 * (y * y * y))))


def _compress_kernel(c0_ref, c1_ref, pos_ref, w1_ref, w2_ref, o_ref):
    n = c0_ref.shape[0] // CMP_STRIDE
    for g, c_ref in enumerate((c0_ref, c1_ref)):
        y_lo = jnp.zeros((n, LANES), F32)
        y_hi = jnp.zeros((n, LANES), F32)
        for j in range(CMP_STRIDE):
            rows = c_ref[pl.ds(j, n, stride=CMP_STRIDE), :]
            y_lo = y_lo + _dot((rows + pos_ref[j]).astype(BF16), w1_ref[j])
            y_hi = y_hi + _dot((rows + pos_ref[CMP_STRIDE + j]).astype(BF16), w1_ref[CMP_STRIDE + j])
        y = y_lo + pltpu.roll(y_hi, n - 1, 0)
        out = _dot(_gelu_tanh(y).astype(BF16), w2_ref[...])
        row = lax.broadcasted_iota(jnp.int32, out.shape, 0)
        o_ref[0, g] = jnp.where(row < n - 1, out, 0.0)


def _compress(cmp0, cmp1, pos, w1, w2, B, S):
    nr = S // CMP_STRIDE
    cspec = pl.BlockSpec((S, LANES), lambda b: (b, 0))
    return pl.pallas_call(
        _compress_kernel,
        grid=(B,),
        in_specs=[cspec, cspec,
                  pl.BlockSpec((CMP_BLOCK, 1, LANES), lambda b: (0, 0, 0)),
                  pl.BlockSpec((CMP_BLOCK, LANES, LANES), lambda b: (0, 0, 0)),
                  pl.BlockSpec((LANES, LANES), lambda b: (0, 0))],
        out_specs=pl.BlockSpec((1, NSA_KV_GROUPS, nr, LANES), lambda b: (b, 0, 0, 0)),
        out_shape=jax.ShapeDtypeStruct((B, NSA_KV_GROUPS, nr, LANES), F32),
        compiler_params=pltpu.CompilerParams(
            dimension_semantics=("parallel",), vmem_limit_bytes=VMEM_LIMIT),
        name="nsa_compress",
    )(cmp0, cmp1, pos, w1, w2)


NSA_WBLK = 2 * LANES
NSA_VPAD = 16


def _tile_heads(x):
    return jnp.concatenate([x] * NSA_HPG, axis=1)


def _nsa_init(slc_ref, win_ref, expand_ref, ks_ref, vst_ref, kw_ref, vwt_ref):
    hd = NSA_HD
    S = slc_ref.shape[0]
    kc = vst_ref.shape[3]
    ones_rows = lambda n: jnp.where(lax.broadcasted_iota(jnp.int32, (NSA_VPAD, n), 0) == 0, 1.0, 0.0)
    for g in range(NSA_KV_GROUPS):
        slab = slc_ref[:, g * LANES:(g + 1) * LANES]
        ks_ref[g, :, :hd] = slab[:, :hd].astype(BF16)
        ks_ref[g, :, hd:] = expand_ref[...]
        for c in range(S // kc):
            vt = slab[c * kc:(c + 1) * kc].T[hd:]
            vst_ref[g, c] = jnp.concatenate([vt, ones_rows(kc)], axis=0).astype(BF16)
        slab = win_ref[:, g * LANES:(g + 1) * LANES]
        kw_ref[g] = slab[:, :hd].astype(BF16)
        for c in range(S // NSA_WBLK):
            vt = slab[c * NSA_WBLK:(c + 1) * NSA_WBLK].T[hd:]
            vwt_ref[g, c] = jnp.concatenate([vt, ones_rows(NSA_WBLK)], axis=0).astype(BF16)


def _nsa_step(q_ref, cmp_ref, gate_ref, o_ref, ks_ref, vst_ref, kw_ref, vwt_ref, m_ref, acc_ref,
              before_loop):
    QT = q_ref.shape[0]
    G, hd, H = NSA_KV_GROUPS, NSA_HD, NSA_HPG
    S = ks_ref.shape[1]
    kc = vst_ref.shape[3]
    qb = pl.program_id(1)
    q0 = qb * QT
    n_slc = S // SLC_BLOCK
    n_cmp = S // CMP_STRIDE
    ns = m_ref.shape[1]
    groups = range(G)

    t_row = q0 + lax.broadcasted_iota(jnp.int32, (1, QT), 1)
    cmp_end = lax.broadcasted_iota(jnp.int32, (n_cmp, 1), 0) * CMP_STRIDE + (CMP_BLOCK - 1)
    bias_c = _tile_heads(jnp.where(cmp_end <= t_row, 0.0, NEG))
    any_c = _tile_heads(jnp.where(t_row >= CMP_BLOCK - 1, 1.0, 0.0))
    wlen = WINDOW + QT
    w0 = pl.multiple_of(jnp.maximum(q0 - WINDOW, 0), NSA_WBLK)
    wb = w0 // NSA_WBLK
    dist = t_row - (w0 + lax.broadcasted_iota(jnp.int32, (wlen, 1), 0))
    bias_w = _tile_heads(jnp.where(dist >= 0, jnp.where(dist < WINDOW, 0.0, NEG), NEG))

    q_cat, q_cat2 = [], []
    for g in groups:
        q_t = (q_ref[:, g * NQ_W:(g + 1) * NQ_W] * (hd ** -0.5)).T
        q_f = jnp.concatenate([q_t[h * hd:(h + 1) * hd] for h in range(H)], axis=1)
        q_cat.append(q_f.astype(BF16))
        q_cat2.append((q_f * LOG2E).astype(BF16))
    s_c = [_dot(cmp_ref[0, g, :, :hd].astype(BF16), q_cat[g]) + bias_c for g in groups]
    s_w = [_dot(kw_ref[g, pl.ds(w0, wlen), :], q_cat2[g]) + bias_w for g in groups]

    sj = lax.broadcasted_iota(jnp.int32, (n_slc, n_cmp), 0) * SLC_BLOCK
    cn = lax.broadcasted_iota(jnp.int32, (n_slc, n_cmp), 1) * CMP_STRIDE
    overlap = jnp.where(cn <= sj + (SLC_BLOCK - 1),
                        jnp.where(cn + (CMP_BLOCK - 1) >= sj, 1.0, 0.0), 0.0).astype(BF16)
    o_c, imp = [], []
    for g in groups:
        e = jnp.exp(s_c[g] - jnp.max(s_c[g], axis=0, keepdims=True))
        p_c = e * (any_c / jnp.maximum(jnp.sum(e, axis=0, keepdims=True), 1e-30))
        o_c.append(_dot(cmp_ref[0, g].T[hd:].astype(BF16), p_c.astype(BF16)))
        p_sum = p_c[:, :QT]
        for h in range(1, H):
            p_sum = p_sum + p_c[:, h * QT:(h + 1) * QT]
        imp.append(_dot_exact_lhs(overlap, p_sum))

    o_w = []
    for g in groups:
        e = jnp.exp2((s_w[g] - jnp.max(s_w[g], axis=0, keepdims=True)).astype(BF16))
        ow = _dot(vwt_ref[g, wb], e[:NSA_WBLK])
        for j in range(1, wlen // NSA_WBLK):
            ow = ow + _dot(vwt_ref[g, wb + j], e[j * NSA_WBLK:(j + 1) * NSA_WBLK])
        o_w.append(ow[:hd] * (1.0 / jnp.maximum(ow[hd:hd + 1], 1e-30)))

    jblk = lax.broadcasted_iota(jnp.int32, (n_slc, QT), 0)
    bt = (q0 + lax.broadcasted_iota(jnp.int32, (n_slc, QT), 1)) // SLC_BLOCK
    back = bt - jblk
    bonus = jnp.where(jblk == 0, FORCE_BONUS,
                      jnp.where(back >= 0, jnp.where(back < FORCED_LOCAL, FORCE_BONUS, 0.0), 0.0))
    jf = jblk.astype(F32)
    q_aug = []
    scores = [jnp.where(jblk <= bt, imp[g] + bonus, NEG) for g in groups]
    sel_biases = [jnp.full((n_slc, QT), NEG, F32) for g in groups]
    for _ in range(min(N_SELECT, n_slc)):
        for g in groups:
            top = jnp.max(scores[g], axis=0, keepdims=True)
            first = jnp.min(jnp.where(scores[g] == top, jf, float(n_slc)), axis=0, keepdims=True)
            hit = jf == first
            sel_biases[g] = jnp.where(hit, 0.0, sel_biases[g])
            scores[g] = jnp.where(hit, -jnp.inf, scores[g])
    for g in groups:
        sel_bias = sel_biases[g].astype(BF16)
        q_aug.append(jnp.concatenate([q_cat2[g], _tile_heads(sel_bias)], axis=0))

    before_loop()
    m_ref[...] = jnp.full_like(m_ref, M_FLOOR)
    acc_ref[...] = jnp.zeros_like(acc_ref)
    gap = (q0 + lax.broadcasted_iota(jnp.int32, (kc, QT), 1)) - lax.broadcasted_iota(jnp.int32, (kc, QT), 0)

    def chunks(it, causal_mask):
        scores = {}
        for j in range(ns):
            c = it * ns + j
            mask = _tile_heads(jnp.where(gap >= c * kc, 0.0, NEG)) if causal_mask else None
            for g in groups:
                s = _dot(ks_ref[g, pl.ds(pl.multiple_of(c * kc, kc), kc), :], q_aug[g])
                scores[g, j] = s + mask if causal_mask else s
        for j in range(ns):
            c = it * ns + j
            for g in groups:
                s = scores[g, j]
                m_old = m_ref[g, j]
                m_new = jnp.maximum(m_old, jnp.max(s, axis=0, keepdims=True))
                alpha = jnp.exp2(m_old - m_new)
                p = jnp.exp2((s - m_new).astype(BF16))
                acc_ref[g, j] = alpha * acc_ref[g, j] + _dot(vst_ref[g, c], p)
                m_ref[g, j] = m_new

    n_full = q0 // (ns * kc)
    lax.fori_loop(0, n_full, lambda it, carry: (chunks(it, False), carry)[1], 0)
    chunks(n_full, True)

    for g in groups:
        m_all = m_ref[g, 0]
        for j in range(1, ns):
            m_all = jnp.maximum(m_all, m_ref[g, j])
        acc = jnp.zeros(acc_ref.shape[2:], F32)
        for j in range(ns):
            acc = acc + jnp.exp2(m_ref[g, j] - m_all) * acc_ref[g, j]
        o_s = acc[:hd] * (1.0 / jnp.maximum(acc[hd:hd + 1], 1e-30))
        sig_t = _sigmoid(gate_ref[:, g * LANES:(g + 1) * LANES]).T
        gate = lambda c: jnp.concatenate([sig_t[c * H + h:c * H + h + 1] for h in range(H)], axis=1)
        out_t = gate(0) * o_c[g] + gate(1) * o_s + gate(2) * o_w[g]
        out = jnp.concatenate([out_t[:, h * QT:(h + 1) * QT] for h in range(H)], axis=0)
        o_ref[:, g * NQ_W:(g + 1) * NQ_W] = out.T.astype(o_ref.dtype)


def _even_mixers_kernel(rq_ref, rk_ref, rv_ref, rg_ref, cos_ref, sin_ref,
                        nq_ref, cmp_ref, slc_ref, win_ref, gate_ref, expand_ref,
                        o_r_ref, o_n_ref,
                        state_ref, dmat_ref, qdec_ref, kdec_ref,
                        ks_ref, vst_ref, kw_ref, vwt_ref, m_ref, acc_ref):
    @pl.when(pl.program_id(1) == 0)
    def _():
        _retention_init(state_ref, dmat_ref, qdec_ref, kdec_ref)
        _nsa_init(slc_ref, win_ref, expand_ref, ks_ref, vst_ref, kw_ref, vwt_ref)

    retention = functools.partial(_retention_step, rq_ref, rk_ref, rv_ref, rg_ref, cos_ref, sin_ref, o_r_ref,
                                  state_ref, dmat_ref, qdec_ref, kdec_ref)
    _nsa_step(nq_ref, cmp_ref, gate_ref, o_n_ref, ks_ref, vst_ref, kw_ref, vwt_ref, m_ref, acc_ref,
              before_loop=retention)


def _even_mixers(main, kvcmp, cosf, sinf, B, S, kc=128, qt=256, ns=4):
    T = B * S
    nqb = S // qt
    G = NSA_KV_GROUPS
    assert S % (ns * kc) == 0 and (ns * kc) % qt == 0 and qt % kc == 0 and qt % NSA_WBLK == 0
    assert S >= WINDOW + qt
    n_slc = S // SLC_BLOCK
    expand = (jnp.arange(S, dtype=jnp.int32)[:, None] // SLC_BLOCK
              == jnp.arange(n_slc, dtype=jnp.int32)[None, :]).astype(BF16)
    gw = G * LANES
    vrows = NSA_HD + NSA_VPAD
    rows = lambda width, col: pl.BlockSpec((qt, width), lambda b, i: (b * nqb + i, col // width))
    seq = lambda width, col: pl.BlockSpec((S, width), lambda b, i: (b, col // width))
    tab = pl.BlockSpec((qt, RET_DK), lambda b, i: (i, 0))
    return pl.pallas_call(
        _even_mixers_kernel,
        grid=(B, nqb),
        in_specs=[
            rows(RET_W, COL_RQ), rows(RET_W, COL_RK), rows(RET_W, COL_RV), rows(RET_W, COL_RG), tab, tab,
            rows(G * NQ_W, COL_NQ),
            pl.BlockSpec((1, G, S // CMP_STRIDE, LANES), lambda b, i: (b, 0, 0, 0)),
            seq(gw, COL_SLC), seq(gw, COL_WIN), rows(gw, COL_GATE),
            pl.BlockSpec((S, n_slc), lambda b, i: (0, 0)),
        ],
        out_specs=[pl.BlockSpec((qt, RET_W), lambda b, i: (b * nqb + i, 0)),
                   pl.BlockSpec((qt, G * NQ_W), lambda b, i: (b * nqb + i, 0))],
        out_shape=[jax.ShapeDtypeStruct((T, RET_W), BF16), jax.ShapeDtypeStruct((T, G * NQ_W), BF16)],
        scratch_shapes=[
            pltpu.VMEM((RET_HEADS, RET_DK, RET_DV), F32),
            pltpu.VMEM((RET_HEADS, qt, qt), F32),
            pltpu.VMEM((RET_HEADS, qt, RET_DK), F32),
            pltpu.VMEM((RET_HEADS, qt, RET_DK), F32),
            pltpu.VMEM((G, S, NSA_HD + n_slc), BF16),
            pltpu.VMEM((G, S // kc, vrows, kc), BF16),
            pltpu.VMEM((G, S, NSA_HD), BF16),
            pltpu.VMEM((G, S // NSA_WBLK, vrows, NSA_WBLK), BF16),
            pltpu.VMEM((G, ns, 1, NSA_HPG * qt), F32),
            pltpu.VMEM((G, ns, vrows, NSA_HPG * qt), F32)],
        compiler_params=pltpu.CompilerParams(
            dimension_semantics=("parallel", "arbitrary"), vmem_limit_bytes=VMEM_LIMIT),
        name="even_mixers",
    )(main, main, main, main, cosf, sinf, main, kvcmp, main, main, main, expand)


HGRN_SUB = HGRN_CHUNK // 2
HGRN_SAFE_RANGE = 60.0


def _hgrn_kernel(q_ref, f_ref, i_ref, g_ref, lbl_ref, ng_ref, o_ref,
                 st_ref, oi_s, oc_s, q_s, k_s, b_s, v_s, *, rb, hp, layer_j):
    C, SB, DK = HGRN_CHUNK, HGRN_SUB, HGRN_DK
    nch = rb // C

    @pl.when(pl.program_id(2) == 0)
    def _():
        st_ref[...] = jnp.zeros_like(st_ref)

    logits = lbl_ref[...]
    e = jnp.exp(logits - jnp.max(logits, axis=0, keepdims=True))
    sm = e / jnp.sum(e, axis=0, keepdims=True)
    lb_all = jnp.maximum(jnp.sum(sm[:layer_j + 1], axis=0, keepdims=True) - sm[0:1], 0.0)

    ri = lax.broadcasted_iota(jnp.int32, (rb, rb), 0)
    ci = lax.broadcasted_iota(jnp.int32, (rb, rb), 1)
    lower = ((ri // C) == (ci // C)) & (ci <= ri)
    same_sub = (ri // SB) == (ci // SB)
    ltri = jnp.where(lower, 1.0, 0.0).astype(BF16)
    cat = lambda xs: jnp.concatenate(xs, axis=0).astype(BF16)

    heads = []
    span = jnp.zeros((1, DK), F32)
    for h in range(hp):
        hs = slice(h * DK, (h + 1) * DK)
        lb = lb_all[:, hs]
        fp = f_ref[:, hs]
        qp = q_ref[:, hs]
        v = i_ref[:, hs]
        log_sig = jnp.minimum(fp, 0.0) - _softplus_neg_abs(fp)
        c_term = jnp.log1p(-lb) + log_sig
        a_term = jnp.log(lb)
        log_f = jnp.maximum(a_term, c_term) + _softplus_neg_abs(a_term - c_term)
        k = (1.0 - lb) * _sigmoid(-fp)
        q = qp * _sigmoid(qp) * (DK ** -0.5)
        b = _dot_exact_lhs(ltri, log_f)
        for s0 in range(0, rb, SB):
            span = jnp.maximum(span, b[s0:s0 + 1] - b[s0 + SB - 1:s0 + SB])
        heads.append((q, k, b, v))

    b2s = [b * LOG2E for _, _, b, _ in heads]
    attns = []
    for (q, k, _, v), b2 in zip(heads, b2s):
        qd, kd, qo, ko = [], [], [], []
        for s0 in range(0, rb, SB):
            bb = b2[s0:s0 + SB]
            ref = b2[s0 + SB // 2:s0 + SB // 2 + 1]
            qd.append(q[s0:s0 + SB] * jnp.exp2(bb - ref))
            kd.append(k[s0:s0 + SB] * jnp.exp2(ref - bb))
            if (s0 // SB) % 2 == 0:
                end = b2[s0 + SB - 1:s0 + SB]
                ko.append(k[s0:s0 + SB] * jnp.exp2(end - bb))
                qo.append(jnp.zeros((SB, DK), F32))
            else:
                end = b2[s0 - 1:s0]
                qo.append(q[s0:s0 + SB] * jnp.exp2(bb - end))
                ko.append(jnp.zeros((SB, DK), F32))
        a_diag = _dot_nt(cat(qd), cat(kd))
        a_off = _dot_nt(cat(qo), cat(ko))
        attns.append(jnp.where(lower, jnp.where(same_sub, a_diag, a_off), 0.0).astype(BF16))

    kvs = []
    for h, ((q, k, _, v), b2) in enumerate(zip(heads, b2s)):
        vb16 = v.astype(BF16)
        oi_s[h] = _dot(attns[h], vb16)
        kv = []
        for c in range(nch):
            sl = slice(c * C, (c + 1) * C)
            b_last = b2[(c + 1) * C - 1:(c + 1) * C]
            kv.append(_dot_tn(vb16[sl], (k[sl] * jnp.exp2(b_last - b2[sl])).astype(BF16)))
        kvs.append(kv)

    for h, ((q, k, _, v), b2) in enumerate(zip(heads, b2s)):
        qb = (q * jnp.exp2(b2)).astype(BF16)
        st = st_ref[h]
        outs = []
        for c in range(nch):
            sl = slice(c * C, (c + 1) * C)
            outs.append(_dot_nt(qb[sl], st.astype(BF16)))
            st = jnp.exp2(b2[(c + 1) * C - 1:(c + 1) * C]) * st + kvs[h][c]
        st_ref[h] = st
        oc_s[h] = jnp.concatenate(outs, axis=0)

    @pl.when(jnp.max(span) > HGRN_SAFE_RANGE)
    def _():
        for h, (q, k, b, v) in enumerate(heads):
            q_s[...] = q
            k_s[...] = k
            b_s[...] = b
            v_s[...] = v

            def row(n, carry):
                c0 = pl.multiple_of((n // C) * C, C)
                qn = q_s[pl.ds(n, 1), :]
                bn = b_s[pl.ds(n, 1), :]
                kb = k_s[pl.ds(c0, C), :]
                bb = b_s[pl.ds(c0, C), :]
                vb = v_s[pl.ds(c0, C), :]
                rows = c0 + lax.broadcasted_iota(jnp.int32, (C, 1), 0)
                decay = jnp.exp(jnp.where(rows <= n, bn - bb, -jnp.inf))
                a = jnp.sum(qn * decay * kb, axis=-1, keepdims=True)
                oi_s[h, pl.ds(n, 1), :] = jnp.sum(a * vb, axis=0, keepdims=True)
                return carry
            lax.fori_loop(0, rb, row, 0)

    for h in range(hp):
        hs = slice(h * DK, (h + 1) * DK)
        o = oi_s[h] + oc_s[h]
        o_ref[:, hs] = (_rms(o, ng_ref[...]) * _sigmoid(g_ref[:, hs])).astype(o_ref.dtype)


def _hgrn(proj, lb_logits, norm_g, layer_j, B, S, rb=256, hp=8):
    T = B * S
    nrb = S // rb
    H = HGRN_HEADS
    ng = H // hp
    n_layers = lb_logits.shape[0]
    w = hp * HGRN_DK
    blk = lambda c: pl.BlockSpec((rb, w), lambda b, h, r, c=c: (b * nrb + r, c * ng + h))
    vm = lambda *lead: pltpu.VMEM((*lead, rb, HGRN_DK), F32)
    return pl.pallas_call(
        functools.partial(_hgrn_kernel, rb=rb, hp=hp, layer_j=layer_j),
        grid=(B, ng, nrb),
        in_specs=[blk(0), blk(1), blk(2), blk(3),
                  pl.BlockSpec((n_layers, w), lambda b, h, r: (0, h)),
                  pl.BlockSpec((1, HGRN_DV), lambda b, h, r: (0, 0))],
        out_specs=pl.BlockSpec((rb, w), lambda b, h, r: (b * nrb + r, h)),
        out_shape=jax.ShapeDtypeStruct((T, H * HGRN_DV), BF16),
        scratch_shapes=[pltpu.VMEM((hp, HGRN_DV, HGRN_DK), F32), vm(hp), vm(hp), vm(), vm(), vm(), vm()],
        compiler_params=pltpu.CompilerParams(
            dimension_semantics=("parallel", "parallel", "arbitrary"), vmem_limit_bytes=VMEM_LIMIT),
        name="hgrn2",
    )(proj, proj, proj, proj, lb_logits, norm_g.reshape(1, HGRN_DV))


def _mix_ffn_kernel(*refs, n_mix, final):
    x_ref = refs[0]
    mix_refs = refs[1:1 + n_mix]
    wo_refs = refs[1 + n_mix:1 + 2 * n_mix]
    g_ref, w1_ref, w3_ref, w2_ref = refs[1 + 2 * n_mix:5 + 2 * n_mix]
    o_ref = refs[-1]
    x1 = x_ref[...]
    for m_ref, wo_ref in zip(mix_refs, wo_refs):
        x1 = x1 + _dot(m_ref[...], wo_ref[...])
    h = _rms(x1, g_ref[...]).astype(BF16)
    a = _dot(h, w1_ref[...])
    u = (a * _sigmoid(a) * _dot(h, w3_ref[...])).astype(BF16)
    y = x1 + _dot(u, w2_ref[...])
    if final:
        y = _rms(y, refs[-2][...])
    o_ref[...] = y


def _mix_ffn(x2, mixes, wo_stack, j, gains, layer, w1s, w3s, w2s, final_g=None, tm=512):
    T, D = x2.shape
    assert T % tm == 0 and sum(m.shape[1] for m in mixes) == wo_stack.shape[1]
    n_mix = len(mixes)
    in_specs = [pl.BlockSpec((tm, D), lambda i: (i, 0))]
    in_specs += [pl.BlockSpec((tm, m.shape[1]), lambda i: (i, 0)) for m in mixes]
    rows = mixes[0].shape[1]
    assert all(m.shape[1] == rows for m in mixes)
    in_specs += [_layer_spec(wo_stack, j, rows, r) for r in range(n_mix)]
    in_specs += [_layer_spec(gains, layer), _layer_spec(w1s, layer), _layer_spec(w3s, layer),
                 _layer_spec(w2s, layer)]
    args = [x2, *mixes, *([wo_stack] * n_mix), gains, w1s, w3s, w2s]
    if final_g is not None:
        in_specs.append(pl.BlockSpec((1, D), lambda i: (0, 0)))
        args.append(final_g.reshape(1, D))
    return pl.pallas_call(
        functools.partial(_mix_ffn_kernel, n_mix=n_mix, final=final_g is not None),
        grid=(T // tm,),
        in_specs=in_specs,
        out_specs=pl.BlockSpec((tm, D), lambda i: (i, 0)),
        out_shape=jax.ShapeDtypeStruct((T, D), F32),
        compiler_params=pltpu.CompilerParams(
            dimension_semantics=("parallel",), vmem_limit_bytes=VMEM_LIMIT),
        name="mix_ffn",
    )(*args)


def _even_w_in_layout(w):
    hd, G = NSA_HD, NSA_KV_GROUPS
    o_kc = 4 * RET_W + NSA_HEADS * hd
    kv = lambda idx, g: w[..., o_kc + idx * G * hd + g * hd:o_kc + idx * G * hd + (g + 1) * hd]
    o_ng = o_kc + 6 * G * hd
    cols = [w[..., :o_kc]]
    for idx_k, idx_v in ((2, 3), (4, 5)):
        for g in range(G):
            cols += [kv(idx_k, g), kv(idx_v, g)]
    for g in range(G):
        gate = [w[..., o_ng + c * NSA_HEADS + g * NSA_HPG:o_ng + c * NSA_HEADS + (g + 1) * NSA_HPG]
                for c in range(3)]
        cols += gate + [jnp.zeros((*w.shape[:-1], LANES - 3 * NSA_HPG), w.dtype)]
    for g in range(G):
        cols += [kv(0, g), kv(1, g)]
    out = jnp.concatenate(cols, axis=-1)
    assert out.shape[-1] == EVEN_N
    return out.astype(BF16)


def _compress_weights(pos_k, w1_k, w2_k, pos_v, w1_v, w2_v):
    hd = NSA_HD
    z = jnp.zeros((CMP_BLOCK, hd, hd), F32)
    w1k = w1_k.reshape(CMP_BLOCK, hd, hd)
    w1v = w1_v.reshape(CMP_BLOCK, hd, hd)
    top = jnp.concatenate([w1k, z], axis=2)
    bot = jnp.concatenate([z, w1v], axis=2)
    w1 = jnp.concatenate([top, bot], axis=1).astype(BF16)
    pos = jnp.concatenate([pos_k, pos_v], axis=1).reshape(CMP_BLOCK, 1, LANES)
    zz = jnp.zeros((hd, hd), F32)
    w2 = jnp.concatenate([jnp.concatenate([w2_k, zz], axis=1),
                          jnp.concatenate([zz, w2_v], axis=1)], axis=0).astype(BF16)
    return pos, w1, w2


def _rotary_tables(S):
    half = RET_DK // 2
    inv = ROPE_BASE ** (-jnp.arange(half, dtype=F32) / half)
    ang = jnp.arange(S, dtype=F32)[:, None] * inv[None, :]
    cos, sin = jnp.cos(ang), jnp.sin(ang)
    return jnp.concatenate([cos, cos], axis=1), jnp.concatenate([-sin, sin], axis=1)


def kernel(x, norm_mix_g, norm_ffn_g, final_norm_g, even_w_in, even_w_out, cmp_pos_k, cmp_w1_k, cmp_w2_k, cmp_pos_v, cmp_w1_v, cmp_w2_v, odd_w_in, odd_w_out, hgrn_norm_g, hgrn_lb_logits, ffn_w1, ffn_w3, ffn_w2):
    B, S, D = x.shape
    x2 = x.reshape(B * S, D)
    cosf, sinf = _rotary_tables(S)
    w_in_even = _even_w_in_layout(even_w_in)
    w_in_odd = odd_w_in.astype(BF16)
    w_out_even = even_w_out.astype(BF16)
    w_out_odd = odd_w_out.astype(BF16)
    w1s, w3s, w2s = ffn_w1.astype(BF16), ffn_w3.astype(BF16), ffn_w2.astype(BF16)
    g_mix = norm_mix_g.reshape(DEPTH, 1, D)
    g_ffn = norm_ffn_g.reshape(DEPTH, 1, D)
    for layer in range(DEPTH):
        j = layer // 2
        if layer % 2 == 0:
            main, cmp0, cmp1 = _norm_proj(x2, g_mix, layer, w_in_even, j, (EVEN_MAIN, LANES, LANES))
            pos, cw1, cw2 = _compress_weights(cmp_pos_k[j], cmp_w1_k[j], cmp_w2_k[j],
                                              cmp_pos_v[j], cmp_w1_v[j], cmp_w2_v[j])
            kvcmp = _compress(cmp0, cmp1, pos, cw1, cw2, B, S)
            o_r, o_n = _even_mixers(main, kvcmp, cosf, sinf, B, S)
            mixes, wo_stack = (o_r, o_n), w_out_even
        else:
            (proj,) = _norm_proj(x2, g_mix, layer, w_in_odd, j, (4 * D,))
            o_h = _hgrn(proj, hgrn_lb_logits, hgrn_norm_g[j], j, B, S)
            mixes, wo_stack = (o_h,), w_out_odd
        x2 = _mix_ffn(x2, mixes, wo_stack, j, g_ffn, layer, w1s, w3s, w2s,
                      final_g=final_norm_g if layer == DEPTH - 1 else None)
    return x2.reshape(B, S, D)
```

```python
import functools
import math

import jax
import jax.numpy as jnp
from jax import lax
from jax.experimental import pallas as pl
from jax.experimental.pallas import tpu as pltpu

F32 = jnp.float32
BF16 = jnp.bfloat16

D_MODEL = 1024
DEPTH = 4
RET_HEADS = 4
RET_DK = 128
RET_DV = 128
ROPE_BASE = 10000.0
NSA_HEADS = 8
NSA_KV_GROUPS = 2
NSA_HPG = NSA_HEADS // NSA_KV_GROUPS
NSA_HD = 64
CMP_BLOCK = 32
CMP_STRIDE = 16
SLC_BLOCK = 64
N_SELECT = 8
FORCED_LOCAL = 2
FORCE_BONUS = 1.0e4
WINDOW = 512
HGRN_HEADS = 8
HGRN_DK = D_MODEL // HGRN_HEADS
HGRN_DV = D_MODEL // HGRN_HEADS
HGRN_CHUNK = 64
D_FF = -(-8 * D_MODEL // (3 * 256)) * 256
EPS = 1e-6

LANES = 128
VMEM_LIMIT = 56 * 1024 * 1024
NEG = -1e30
M_FLOOR = -1e29
LOG2E = math.log2(math.e)

RET_W = RET_HEADS * RET_DK
COL_RQ, COL_RK, COL_RV, COL_RG = 0, RET_W, 2 * RET_W, 3 * RET_W
COL_NQ = 4 * RET_W
NQ_W = NSA_HPG * NSA_HD
COL_SLC = COL_NQ + NSA_KV_GROUPS * NQ_W
COL_WIN = COL_SLC + NSA_KV_GROUPS * LANES
COL_GATE = COL_WIN + NSA_KV_GROUPS * LANES
EVEN_MAIN = COL_GATE + NSA_KV_GROUPS * LANES
EVEN_N = EVEN_MAIN + NSA_KV_GROUPS * LANES


def _dot(a, b):
    return jnp.dot(a, b, preferred_element_type=F32)


def _dot_nt(a, b):
    return lax.dot_general(a, b, (((1,), (1,)), ((), ())), preferred_element_type=F32)


def _dot_tn(a, b):
    return lax.dot_general(a, b, (((0,), (0,)), ((), ())), preferred_element_type=F32)


def _split3(x):
    hi = x.astype(BF16)
    r = x - hi.astype(F32)
    mid = r.astype(BF16)
    lo = (r - mid.astype(F32)).astype(BF16)
    return hi, mid, lo


def _dot_exact_lhs(sel, x):
    hi, mid, lo = _split3(x)
    return _dot(sel, hi) + _dot(sel, mid) + _dot(sel, lo)


def _rms(x, g):
    return x * lax.rsqrt(jnp.mean(x * x, axis=-1, keepdims=True) + EPS) * g


def _sigmoid(x):
    return 1.0 / (1.0 + jnp.exp(-x))


def _softplus_neg_abs(x):
    return jnp.log2(1.0 + jnp.exp2(jnp.abs(x) * (-LOG2E))) * math.log(2.0)


def _norm_proj_kernel(x_ref, g_ref, w_ref, *o_refs):
    h = _rms(x_ref[...], g_ref[...]).astype(BF16)
    off = 0
    for o_ref in o_refs:
        wd = o_ref.shape[1]
        o_ref[...] = _dot(h, w_ref[:, off:off + wd])
        off += wd


def _layer_spec(stack, layer, rows=None, row_block=0):
    _, r, c = stack.shape
    rows = r if rows is None else rows
    return pl.BlockSpec((None, rows, c), lambda *_: (layer, row_block, 0), pipeline_mode=pl.Buffered(1))


def _norm_proj(x2, gains, layer, w_stack, j, widths, tm=1024):
    T, D = x2.shape
    N = w_stack.shape[2]
    assert sum(widths) == N and T % tm == 0
    return pl.pallas_call(
        _norm_proj_kernel,
        grid=(T // tm,),
        in_specs=[pl.BlockSpec((tm, D), lambda i: (i, 0)), _layer_spec(gains, layer), _layer_spec(w_stack, j)],
        out_specs=[pl.BlockSpec((tm, wd), lambda i: (i, 0)) for wd in widths],
        out_shape=[jax.ShapeDtypeStruct((T, wd), F32) for wd in widths],
        compiler_params=pltpu.CompilerParams(
            dimension_semantics=("parallel",), vmem_limit_bytes=VMEM_LIMIT),
        name="norm_proj",
    )(x2, gains, w_stack)


RET_LOG_GAMMA = [math.log(1.0 - 2.0 ** (-5.0 - h)) for h in range(RET_HEADS)]


def _retention_init(state_ref, dmat_ref, qdec_ref, kdec_ref):
    cb = dmat_ref.shape[1]
    state_ref[...] = jnp.zeros_like(state_ref)
    ri = lax.broadcasted_iota(jnp.int32, (cb, cb), 0)
    ci = lax.broadcasted_iota(jnp.int32, (cb, cb), 1)
    diff = (ri - ci).astype(F32)
    causal = ri >= ci
    idx = lax.broadcasted_iota(jnp.int32, (cb, RET_DK), 0).astype(F32)
    for h, lg in enumerate(RET_LOG_GAMMA):
        dmat_ref[h] = jnp.where(causal, jnp.exp(lg * jnp.where(causal, diff, 0.0)), 0.0)
        qdec_ref[h] = jnp.exp(lg * (idx + 1.0))
        kdec_ref[h] = jnp.exp(lg * (cb - 1.0 - idx))


def _retention_step(q_ref, k_ref, v_ref, g_ref, cos_ref, sin_ref, o_ref,
                    state_ref, dmat_ref, qdec_ref, kdec_ref):
    cb = dmat_ref.shape[1]
    cosf = cos_ref[...]
    sinf = sin_ref[...]
    for h, lg in enumerate(RET_LOG_GAMMA):
        sl = slice(h * RET_DK, (h + 1) * RET_DK)
        q = q_ref[:, sl]
        k = k_ref[:, sl]
        v = v_ref[:, sl].astype(BF16)
        qr = q * cosf + pltpu.roll(q, RET_DK // 2, 1) * sinf
        kr = (k * cosf + pltpu.roll(k, RET_DK // 2, 1) * sinf) * (RET_DK ** -0.5)
        scores = _dot_nt(qr.astype(BF16), kr.astype(BF16)) * dmat_ref[h]
        o = _dot(scores.astype(BF16), v)
        q_dec = qr * qdec_ref[h]
        state = state_ref[h]
        o = o + _dot(q_dec.astype(BF16), state.astype(BF16))
        k_dec = kr * kdec_ref[h]
        state_ref[h] = math.exp(lg * cb) * state + _dot_tn(k_dec.astype(BF16), v)
        mu = jnp.mean(o, axis=-1, keepdims=True)
        oc = o - mu
        var = jnp.mean(oc * oc, axis=-1, keepdims=True)
        gate = g_ref[:, sl]
        o_ref[:, sl] = (oc * lax.rsqrt(var + 1e-5) * (gate * _sigmoid(gate))).astype(o_ref.dtype)


def _gelu_tanh(y):
    return 0.5 * y * (1.0 + jnp.tanh(math.sqrt(2.0 / math.pi) * (y + 0.044715 * (y * y * y))))


def _compress_kernel(c0_ref, c1_ref, pos_ref, w1_ref, w2_ref, o_ref):
    n = c0_ref.shape[0] // CMP_STRIDE
    for g, c_ref in enumerate((c0_ref, c1_ref)):
        y_lo = jnp.zeros((n, LANES), F32)
        y_hi = jnp.zeros((n, LANES), F32)
        for j in range(CMP_STRIDE):
            rows = c_ref[pl.ds(j, n, stride=CMP_STRIDE), :]
            y_lo = y_lo + _dot((rows + pos_ref[j]).astype(BF16), w1_ref[j])
            y_hi = y_hi + _dot((rows + pos_ref[CMP_STRIDE + j]).astype(BF16), w1_ref[CMP_STRIDE + j])
        y = y_lo + pltpu.roll(y_hi, n - 1, 0)
        out = _dot(_gelu_tanh(y).astype(BF16), w2_ref[...])
        row = lax.broadcasted_iota(jnp.int32, out.shape, 0)
        o_ref[0, g] = jnp.where(row < n - 1, out, 0.0)


def _compress(cmp0, cmp1, pos, w1, w2, B, S):
    nr = S // CMP_STRIDE
    cspec = pl.BlockSpec((S, LANES), lambda b: (b, 0))
    return pl.pallas_call(
        _compress_kernel,
        grid=(B,),
        in_specs=[cspec, cspec,
                  pl.BlockSpec((CMP_BLOCK, 1, LANES), lambda b: (0, 0, 0)),
                  pl.BlockSpec((CMP_BLOCK, LANES, LANES), lambda b: (0, 0, 0)),
                  pl.BlockSpec((LANES, LANES), lambda b: (0, 0))],
        out_specs=pl.BlockSpec((1, NSA_KV_GROUPS, nr, LANES), lambda b: (b, 0, 0, 0)),
        out_shape=jax.ShapeDtypeStruct((B, NSA_KV_GROUPS, nr, LANES), F32),
        compiler_params=pltpu.CompilerParams(
            dimension_semantics=("parallel",), vmem_limit_bytes=VMEM_LIMIT),
        name="nsa_compress",
    )(cmp0, cmp1, pos, w1, w2)


NSA_WBLK = 2 * LANES
NSA_VPAD = 16


def _tile_heads(x):
    return jnp.concatenate([x] * NSA_HPG, axis=1)


def _nsa_init(slc_ref, win_ref, expand_ref, ks_ref, vst_ref, kw_ref, vwt_ref):
    hd = NSA_HD
    S = slc_ref.shape[0]
    kc = vst_ref.shape[3]
    ones_rows = lambda n: jnp.where(lax.broadcasted_iota(jnp.int32, (NSA_VPAD, n), 0) == 0, 1.0, 0.0)
    for g in range(NSA_KV_GROUPS):
        slab = slc_ref[:, g * LANES:(g + 1) * LANES]
        ks_ref[g, :, :hd] = slab[:, :hd].astype(BF16)
        ks_ref[g, :, hd:] = expand_ref[...]
        for c in range(S // kc):
            vt = slab[c * kc:(c + 1) * kc].T[hd:]
            vst_ref[g, c] = jnp.concatenate([vt, ones_rows(kc)], axis=0).astype(BF16)
        slab = win_ref[:, g * LANES:(g + 1) * LANES]
        kw_ref[g] = slab[:, :hd].astype(BF16)
        for c in range(S // NSA_WBLK):
            vt = slab[c * NSA_WBLK:(c + 1) * NSA_WBLK].T[hd:]
            vwt_ref[g, c] = jnp.concatenate([vt, ones_rows(NSA_WBLK)], axis=0).astype(BF16)


def _nsa_step(q_ref, cmp_ref, gate_ref, o_ref, ks_ref, vst_ref, kw_ref, vwt_ref, m_ref, acc_ref,
              before_loop):
    QT = q_ref.shape[0]
    G, hd, H = NSA_KV_GROUPS, NSA_HD, NSA_HPG
    S = ks_ref.shape[1]
    kc = vst_ref.shape[3]
    qb = pl.program_id(1)
    q0 = qb * QT
    n_slc = S // SLC_BLOCK
    n_cmp = S // CMP_STRIDE
    ns = m_ref.shape[1]
    groups = range(G)

    t_row = q0 + lax.broadcasted_iota(jnp.int32, (1, QT), 1)
    cmp_end = lax.broadcasted_iota(jnp.int32, (n_cmp, 1), 0) * CMP_STRIDE + (CMP_BLOCK - 1)
    bias_c = _tile_heads(jnp.where(cmp_end <= t_row, 0.0, NEG))
    any_c = _tile_heads(jnp.where(t_row >= CMP_BLOCK - 1, 1.0, 0.0))
    wlen = WINDOW + QT
    w0 = pl.multiple_of(jnp.maximum(q0 - WINDOW, 0), NSA_WBLK)
    wb = w0 // NSA_WBLK
    dist = t_row - (w0 + lax.broadcasted_iota(jnp.int32, (wlen, 1), 0))
    bias_w = _tile_heads(jnp.where(dist >= 0, jnp.where(dist < WINDOW, 0.0, NEG), NEG))

    q_cat, q_cat2 = [], []
    for g in groups:
        q_t = (q_ref[:, g * NQ_W:(g + 1) * NQ_W] * (hd ** -0.5)).T
        q_f = jnp.concatenate([q_t[h * hd:(h + 1) * hd] for h in range(H)], axis=1)
        q_cat.append(q_f.astype(BF16))
        q_cat2.append((q_f * LOG2E).astype(BF16))
    s_c = [_dot(cmp_ref[0, g, :, :hd].astype(BF16), q_cat[g]) + bias_c for g in groups]
    s_w = [_dot(kw_ref[g, pl.ds(w0, wlen), :], q_cat2[g]) + bias_w for g in groups]

    sj = lax.broadcasted_iota(jnp.int32, (n_slc, n_cmp), 0) * SLC_BLOCK
    cn = lax.broadcasted_iota(jnp.int32, (n_slc, n_cmp), 1) * CMP_STRIDE
    overlap = jnp.where(cn <= sj + (SLC_BLOCK - 1),
                        jnp.where(cn + (CMP_BLOCK - 1) >= sj, 1.0, 0.0), 0.0).astype(BF16)
    o_c, imp = [], []
    for g in groups:
        e = jnp.exp(s_c[g] - jnp.max(s_c[g], axis=0, keepdims=True))
        p_c = e * (any_c / jnp.maximum(jnp.sum(e, axis=0, keepdims=True), 1e-30))
        o_c.append(_dot(cmp_ref[0, g].T[hd:].astype(BF16), p_c.astype(BF16)))
        p_sum = p_c[:, :QT]
        for h in range(1, H):
            p_sum = p_sum + p_c[:, h * QT:(h + 1) * QT]
        imp.append(_dot_exact_lhs(overlap, p_sum))

    o_w = []
    for g in groups:
        e = jnp.exp2((s_w[g] - jnp.max(s_w[g], axis=0, keepdims=True)).astype(BF16))
        ow = _dot(vwt_ref[g, wb], e[:NSA_WBLK])
        for j in range(1, wlen // NSA_WBLK):
            ow = ow + _dot(vwt_ref[g, wb + j], e[j * NSA_WBLK:(j + 1) * NSA_WBLK])
        o_w.append(ow[:hd] * (1.0 / jnp.maximum(ow[hd:hd + 1], 1e-30)))

    jblk = lax.broadcasted_iota(jnp.int32, (n_slc, QT), 0)
    bt = (q0 + lax.broadcasted_iota(jnp.int32, (n_slc, QT), 1)) // SLC_BLOCK
    back = bt - jblk
    bonus = jnp.where(jblk == 0, FORCE_BONUS,
                      jnp.where(back >= 0, jnp.where(back < FORCED_LOCAL, FORCE_BONUS, 0.0), 0.0))
    jf = jblk.astype(F32)
    q_aug = []
    scores = [jnp.where(jblk <= bt, imp[g] + bonus, NEG) for g in groups]
    sel_biases = [jnp.full((n_slc, QT), NEG, F32) for g in groups]
    for _ in range(min(N_SELECT, n_slc)):
        for g in groups:
            top = jnp.max(scores[g], axis=0, keepdims=True)
            first = jnp.min(jnp.where(scores[g] == top, jf, float(n_slc)), axis=0, keepdims=True)
            hit = jf == first
            sel_biases[g] = jnp.where(hit, 0.0, sel_biases[g])
            scores[g] = jnp.where(hit, -jnp.inf, scores[g])
    for g in groups:
        sel_bias = sel_biases[g].astype(BF16)
        q_aug.append(jnp.concatenate([q_cat2[g], _tile_heads(sel_bias)], axis=0))

    before_loop()
    m_ref[...] = jnp.full_like(m_ref, M_FLOOR)
    acc_ref[...] = jnp.zeros_like(acc_ref)
    gap = (q0 + lax.broadcasted_iota(jnp.int32, (kc, QT), 1)) - lax.broadcasted_iota(jnp.int32, (kc, QT), 0)

    def chunks(it, causal_mask):
        scores = {}
        for j in range(ns):
            c = it * ns + j
            mask = _tile_heads(jnp.where(gap >= c * kc, 0.0, NEG)) if causal_mask else None
            for g in groups:
                s = _dot(ks_ref[g, pl.ds(pl.multiple_of(c * kc, kc), kc), :], q_aug[g])
                scores[g, j] = s + mask if causal_mask else s
        for j in range(ns):
            c = it * ns + j
            for g in groups:
                s = scores[g, j]
                m_old = m_ref[g, j]
                m_new = jnp.maximum(m_old, jnp.max(s, axis=0, keepdims=True))
                alpha = jnp.exp2(m_old - m_new)
                p = jnp.exp2((s - m_new).astype(BF16))
                acc_ref[g, j] = alpha * acc_ref[g, j] + _dot(vst_ref[g, c], p)
                m_ref[g, j] = m_new

    n_full = q0 // (ns * kc)
    lax.fori_loop(0, n_full, lambda it, carry: (chunks(it, False), carry)[1], 0)
    chunks(n_full, True)

    for g in groups:
        m_all = m_ref[g, 0]
        for j in range(1, ns):
            m_all = jnp.maximum(m_all, m_ref[g, j])
        acc = jnp.zeros(acc_ref.shape[2:], F32)
        for j in range(ns):
            acc = acc + jnp.exp2(m_ref[g, j] - m_all) * acc_ref[g, j]
        o_s = acc[:hd] * (1.0 / jnp.maximum(acc[hd:hd + 1], 1e-30))
        sig_t = _sigmoid(gate_ref[:, g * LANES:(g + 1) * LANES]).T
        gate = lambda c: jnp.concatenate([sig_t[c * H + h:c * H + h + 1] for h in range(H)], axis=1)
        out_t = gate(0) * o_c[g] + gate(1) * o_s + gate(2) * o_w[g]
        out = jnp.concatenate([out_t[:, h * QT:(h + 1) * QT] for h in range(H)], axis=0)
        o_ref[:, g * NQ_W:(g + 1) * NQ_W] = out.T.astype(o_ref.dtype)


def _even_mixers_kernel(rq_ref, rk_ref, rv_ref, rg_ref, cos_ref, sin_ref,
                        nq_ref, cmp_ref, slc_ref, win_ref, gate_ref, expand_ref,
                        o_r_ref, o_n_ref,
                        state_ref, dmat_ref, qdec_ref, kdec_ref,
                        ks_ref, vst_ref, kw_ref, vwt_ref, m_ref, acc_ref):
    @pl.when(pl.program_id(1) == 0)
    def _():
        _retention_init(state_ref, dmat_ref, qdec_ref, kdec_ref)
        _nsa_init(slc_ref, win_ref, expand_ref, ks_ref, vst_ref, kw_ref, vwt_ref)

    retention = functools.partial(_retention_step, rq_ref, rk_ref, rv_ref, rg_ref, cos_ref, sin_ref, o_r_ref,
                                  state_ref, dmat_ref, qdec_ref, kdec_ref)
    _nsa_step(nq_ref, cmp_ref, gate_ref, o_n_ref, ks_ref, vst_ref, kw_ref, vwt_ref, m_ref, acc_ref,
              before_loop=retention)


def _even_mixers(main, kvcmp, cosf, sinf, B, S, kc=128, qt=256, ns=4):
    T = B * S
    nqb = S // qt
    G = NSA_KV_GROUPS
    assert S % (ns * kc) == 0 and (ns * kc) % qt == 0 and qt % kc == 0 and qt % NSA_WBLK == 0
    assert S >= WINDOW + qt
    n_slc = S // SLC_BLOCK
    expand = (jnp.arange(S, dtype=jnp.int32)[:, None] // SLC_BLOCK
              == jnp.arange(n_slc, dtype=jnp.int32)[None, :]).astype(BF16)
    gw = G * LANES
    vrows = NSA_HD + NSA_VPAD
    rows = lambda width, col: pl.BlockSpec((qt, width), lambda b, i: (b * nqb + i, col // width))
    seq = lambda width, col: pl.BlockSpec((S, width), lambda b, i: (b, col // width))
    tab = pl.BlockSpec((qt, RET_DK), lambda b, i: (i, 0))
    return pl.pallas_call(
        _even_mixers_kernel,
        grid=(B, nqb),
        in_specs=[
            rows(RET_W, COL_RQ), rows(RET_W, COL_RK), rows(RET_W, COL_RV), rows(RET_W, COL_RG), tab, tab,
            rows(G * NQ_W, COL_NQ),
            pl.BlockSpec((1, G, S // CMP_STRIDE, LANES), lambda b, i: (b, 0, 0, 0)),
            seq(gw, COL_SLC), seq(gw, COL_WIN), rows(gw, COL_GATE),
            pl.BlockSpec((S, n_slc), lambda b, i: (0, 0)),
        ],
        out_specs=[pl.BlockSpec((qt, RET_W), lambda b, i: (b * nqb + i, 0)),
                   pl.BlockSpec((qt, G * NQ_W), lambda b, i: (b * nqb + i, 0))],
        out_shape=[jax.ShapeDtypeStruct((T, RET_W), BF16), jax.ShapeDtypeStruct((T, G * NQ_W), BF16)],
        scratch_shapes=[
            pltpu.VMEM((RET_HEADS, RET_DK, RET_DV), F32),
            pltpu.VMEM((RET_HEADS, qt, qt), F32),
            pltpu.VMEM((RET_HEADS, qt, RET_DK), F32),
            pltpu.VMEM((RET_HEADS, qt, RET_DK), F32),
            pltpu.VMEM((G, S, NSA_HD + n_slc), BF16),
            pltpu.VMEM((G, S // kc, vrows, kc), BF16),
            pltpu.VMEM((G, S, NSA_HD), BF16),
            pltpu.VMEM((G, S // NSA_WBLK, vrows, NSA_WBLK), BF16),
            pltpu.VMEM((G, ns, 1, NSA_HPG * qt), F32),
            pltpu.VMEM((G, ns, vrows, NSA_HPG * qt), F32)],
        compiler_params=pltpu.CompilerParams(
            dimension_semantics=("parallel", "arbitrary"), vmem_limit_bytes=VMEM_LIMIT),
        name="even_mixers",
    )(main, main, main, main, cosf, sinf, main, kvcmp, main, main, main, expand)


HGRN_SUB = HGRN_CHUNK // 2
HGRN_SAFE_RANGE = 60.0


def _hgrn_kernel(q_ref, f_ref, i_ref, g_ref, lbl_ref, ng_ref, o_ref,
                 st_ref, oi_s, oc_s, q_s, k_s, b_s, v_s, *, rb, hp, layer_j):
    C, SB, DK = HGRN_CHUNK, HGRN_SUB, HGRN_DK
    nch = rb // C

    @pl.when(pl.program_id(2) == 0)
    def _():
        st_ref[...] = jnp.zeros_like(st_ref)

    logits = lbl_ref[...]
    e = jnp.exp(logits - jnp.max(logits, axis=0, keepdims=True))
    sm = e / jnp.sum(e, axis=0, keepdims=True)
    lb_all = jnp.maximum(jnp.sum(sm[:layer_j + 1], axis=0, keepdims=True) - sm[0:1], 0.0)

    ri = lax.broadcasted_iota(jnp.int32, (rb, rb), 0)
    ci = lax.broadcasted_iota(jnp.int32, (rb, rb), 1)
    lower = ((ri // C) == (ci // C)) & (ci <= ri)
    ltri = jnp.where(lower, 1.0, 0.0).astype(BF16)
    cat = lambda xs: jnp.concatenate(xs, axis=0).astype(BF16)

    heads = []
    span = jnp.zeros((1, DK), F32)
    for h in range(hp):
        hs = slice(h * DK, (h + 1) * DK)
        lb = lb_all[:, hs]
        fp = f_ref[:, hs]
        qp = q_ref[:, hs]
        v = i_ref[:, hs]
        log_sig = jnp.minimum(fp, 0.0) - _softplus_neg_abs(fp)
        c_term = jnp.log1p(-lb) + log_sig
        a_term = jnp.log(lb)
        log_f = jnp.maximum(a_term, c_term) + _softplus_neg_abs(a_term - c_term)
        k = (1.0 - lb) * _sigmoid(-fp)
        q = qp * _sigmoid(qp) * (DK ** -0.5)
        b = _dot_exact_lhs(ltri, log_f)
        for s0 in range(0, rb, SB):
            span = jnp.maximum(span, b[s0:s0 + 1] - b[s0 + SB - 1:s0 + SB])
        heads.append((q, k, b, v))

    b2s = [b * LOG2E for _, _, b, _ in heads]
    attns = []
    for (q, k, _, v), b2 in zip(heads, b2s):
        zero = jnp.zeros((SB, DK), F32)
        qs, ks = ([], [], []), ([], [], [])
        for s0 in range(0, rb, SB):
            bb = b2[s0:s0 + SB]
            ref = b2[s0 + SB // 2:s0 + SB // 2 + 1]
            qd = q[s0:s0 + SB] * jnp.exp2(bb - ref)
            kd = k[s0:s0 + SB] * jnp.exp2(ref - bb)
            if (s0 // SB) % 2 == 0:
                end = b2[s0 + SB - 1:s0 + SB]
                q_parts = (qd, zero, zero)
                k_parts = (kd, zero, k[s0:s0 + SB] * jnp.exp2(end - bb))
            else:
                end = b2[s0 - 1:s0]
                q_parts = (zero, qd, q[s0:s0 + SB] * jnp.exp2(bb - end))
                k_parts = (zero, kd, zero)
            for dst, part in zip(qs + ks, q_parts + k_parts):
                dst.append(part)
        lhs = jnp.concatenate([cat(p) for p in qs], axis=1)
        rhs = jnp.concatenate([cat(p) for p in ks], axis=1)
        attns.append(jnp.where(lower, _dot_nt(lhs, rhs), 0.0).astype(BF16))

    kvs = []
    for h, ((q, k, _, v), b2) in enumerate(zip(heads, b2s)):
        vb16 = v.astype(BF16)
        oi_s[h] = _dot(attns[h], vb16)
        kv = []
        for c in range(nch):
            sl = slice(c * C, (c + 1) * C)
            b_last = b2[(c + 1) * C - 1:(c + 1) * C]
            kv.append(_dot_tn(vb16[sl], (k[sl] * jnp.exp2(b_last - b2[sl])).astype(BF16)))
        kvs.append(kv)

    for h, ((q, k, _, v), b2) in enumerate(zip(heads, b2s)):
        qb = (q * jnp.exp2(b2)).astype(BF16)
        st = st_ref[h]
        outs = []
        for c in range(nch):
            sl = slice(c * C, (c + 1) * C)
            outs.append(_dot_nt(qb[sl], st.astype(BF16)))
            st = jnp.exp2(b2[(c + 1) * C - 1:(c + 1) * C]) * st + kvs[h][c]
        st_ref[h] = st
        oc_s[h] = jnp.concatenate(outs, axis=0)

    @pl.when(jnp.max(span) > HGRN_SAFE_RANGE)
    def _():
        for h, (q, k, b, v) in enumerate(heads):
            q_s[...] = q
            k_s[...] = k
            b_s[...] = b
            v_s[...] = v

            def row(n, carry):
                c0 = pl.multiple_of((n // C) * C, C)
                qn = q_s[pl.ds(n, 1), :]
                bn = b_s[pl.ds(n, 1), :]
                kb = k_s[pl.ds(c0, C), :]
                bb = b_s[pl.ds(c0, C), :]
                vb = v_s[pl.ds(c0, C), :]
                rows = c0 + lax.broadcasted_iota(jnp.int32, (C, 1), 0)
                decay = jnp.exp(jnp.where(rows <= n, bn - bb, -jnp.inf))
                a = jnp.sum(qn * decay * kb, axis=-1, keepdims=True)
                oi_s[h, pl.ds(n, 1), :] = jnp.sum(a * vb, axis=0, keepdims=True)
                return carry
            lax.fori_loop(0, rb, row, 0)

    for h in range(hp):
        hs = slice(h * DK, (h + 1) * DK)
        o = oi_s[h] + oc_s[h]
        o_ref[:, hs] = (_rms(o, ng_ref[...]) * _sigmoid(g_ref[:, hs])).astype(o_ref.dtype)


def _hgrn(proj, lb_logits, norm_g, layer_j, B, S, rb=256, hp=8):
    T = B * S
    nrb = S // rb
    H = HGRN_HEADS
    ng = H // hp
    n_layers = lb_logits.shape[0]
    w = hp * HGRN_DK
    blk = lambda c: pl.BlockSpec((rb, w), lambda b, h, r, c=c: (b * nrb + r, c * ng + h))
    vm = lambda *lead: pltpu.VMEM((*lead, rb, HGRN_DK), F32)
    return pl.pallas_call(
        functools.partial(_hgrn_kernel, rb=rb, hp=hp, layer_j=layer_j),
        grid=(B, ng, nrb),
        in_specs=[blk(0), blk(1), blk(2), blk(3),
                  pl.BlockSpec((n_layers, w), lambda b, h, r: (0, h)),
                  pl.BlockSpec((1, HGRN_DV), lambda b, h, r: (0, 0))],
        out_specs=pl.BlockSpec((rb, w), lambda b, h, r: (b * nrb + r, h)),
        out_shape=jax.ShapeDtypeStruct((T, H * HGRN_DV), BF16),
        scratch_shapes=[pltpu.VMEM((hp, HGRN_DV, HGRN_DK), F32), vm(hp), vm(hp), vm(), vm(), vm(), vm()],
        compiler_params=pltpu.CompilerParams(
            dimension_semantics=("parallel", "parallel", "arbitrary"), vmem_limit_bytes=VMEM_LIMIT),
        name="hgrn2",
    )(proj, proj, proj, proj, lb_logits, norm_g.reshape(1, HGRN_DV))


def _mix_ffn_kernel(*refs, n_mix, final):
    x_ref = refs[0]
    mix_refs = refs[1:1 + n_mix]
    wo_refs = refs[1 + n_mix:1 + 2 * n_mix]
    g_ref, w1_ref, w3_ref, w2_ref = refs[1 + 2 * n_mix:5 + 2 * n_mix]
    o_ref = refs[-1]
    x1 = x_ref[...]
    for m_ref, wo_ref in zip(mix_refs, wo_refs):
        x1 = x1 + _dot(m_ref[...], wo_ref[...])
    h = _rms(x1, g_ref[...]).astype(BF16)
    a = _dot(h, w1_ref[...])
    u = (a * _sigmoid(a) * _dot(h, w3_ref[...])).astype(BF16)
    y = x1 + _dot(u, w2_ref[...])
    if final:
        y = _rms(y, refs[-2][...])
    o_ref[...] = y


def _mix_ffn(x2, mixes, wo_stack, j, gains, layer, w1s, w3s, w2s, final_g=None, tm=512):
    T, D = x2.shape
    assert T % tm == 0 and sum(m.shape[1] for m in mixes) == wo_stack.shape[1]
    n_mix = len(mixes)
    in_specs = [pl.BlockSpec((tm, D), lambda i: (i, 0))]
    in_specs += [pl.BlockSpec((tm, m.shape[1]), lambda i: (i, 0)) for m in mixes]
    rows = mixes[0].shape[1]
    assert all(m.shape[1] == rows for m in mixes)
    in_specs += [_layer_spec(wo_stack, j, rows, r) for r in range(n_mix)]
    in_specs += [_layer_spec(gains, layer), _layer_spec(w1s, layer), _layer_spec(w3s, layer),
                 _layer_spec(w2s, layer)]
    args = [x2, *mixes, *([wo_stack] * n_mix), gains, w1s, w3s, w2s]
    if final_g is not None:
        in_specs.append(pl.BlockSpec((1, D), lambda i: (0, 0)))
        args.append(final_g.reshape(1, D))
    return pl.pallas_call(
        functools.partial(_mix_ffn_kernel, n_mix=n_mix, final=final_g is not None),
        grid=(T // tm,),
        in_specs=in_specs,
        out_specs=pl.BlockSpec((tm, D), lambda i: (i, 0)),
        out_shape=jax.ShapeDtypeStruct((T, D), F32),
        compiler_params=pltpu.CompilerParams(
            dimension_semantics=("parallel",), vmem_limit_bytes=VMEM_LIMIT),
        name="mix_ffn",
    )(*args)


def _even_w_in_layout(w):
    hd, G = NSA_HD, NSA_KV_GROUPS
    o_kc = 4 * RET_W + NSA_HEADS * hd
    kv = lambda idx, g: w[..., o_kc + idx * G * hd + g * hd:o_kc + idx * G * hd + (g + 1) * hd]
    o_ng = o_kc + 6 * G * hd
    cols = [w[..., :o_kc]]
    for idx_k, idx_v in ((2, 3), (4, 5)):
        for g in range(G):
            cols += [kv(idx_k, g), kv(idx_v, g)]
    for g in range(G):
        gate = [w[..., o_ng + c * NSA_HEADS + g * NSA_HPG:o_ng + c * NSA_HEADS + (g + 1) * NSA_HPG]
                for c in range(3)]
        cols += gate + [jnp.zeros((*w.shape[:-1], LANES - 3 * NSA_HPG), w.dtype)]
    for g in range(G):
        cols += [kv(0, g), kv(1, g)]
    out = jnp.concatenate(cols, axis=-1)
    assert out.shape[-1] == EVEN_N
    return out.astype(BF16)


def _compress_weights(pos_k, w1_k, w2_k, pos_v, w1_v, w2_v):
    hd = NSA_HD
    z = jnp.zeros((CMP_BLOCK, hd, hd), F32)
    w1k = w1_k.reshape(CMP_BLOCK, hd, hd)
    w1v = w1_v.reshape(CMP_BLOCK, hd, hd)
    top = jnp.concatenate([w1k, z], axis=2)
    bot = jnp.concatenate([z, w1v], axis=2)
    w1 = jnp.concatenate([top, bot], axis=1).astype(BF16)
    pos = jnp.concatenate([pos_k, pos_v], axis=1).reshape(CMP_BLOCK, 1, LANES)
    zz = jnp.zeros((hd, hd), F32)
    w2 = jnp.concatenate([jnp.concatenate([w2_k, zz], axis=1),
                          jnp.concatenate([zz, w2_v], axis=1)], axis=0).astype(BF16)
    return pos, w1, w2


def _rotary_tables(S):
    half = RET_DK // 2
    inv = ROPE_BASE ** (-jnp.arange(half, dtype=F32) / half)
    ang = jnp.arange(S, dtype=F32)[:, None] * inv[None, :]
    cos, sin = jnp.cos(ang), jnp.sin(ang)
    return jnp.concatenate([cos, cos], axis=1), jnp.concatenate([-sin, sin], axis=1)


def kernel(x, norm_mix_g, norm_ffn_g, final_norm_g, even_w_in, even_w_out, cmp_pos_k, cmp_w1_k, cmp_w2_k, cmp_pos_v, cmp_w1_v, cmp_w2_v, odd_w_in, odd_w_out, hgrn_norm_g, hgrn_lb_logits, ffn_w1, ffn_w3, ffn_w2):
    B, S, D = x.shape
    x2 = x.reshape(B * S, D)
    cosf, sinf = _rotary_tables(S)
    w_in_even = _even_w_in_layout(even_w_in)
    w_in_odd = odd_w_in.astype(BF16)
    w_out_even = even_w_out.astype(BF16)
    w_out_odd = odd_w_out.astype(BF16)
    w1s, w3s, w2s = ffn_w1.astype(BF16), ffn_w3.astype(BF16), ffn_w2.astype(BF16)
    g_mix = norm_mix_g.reshape(DEPTH, 1, D)
    g_ffn = norm_ffn_g.reshape(DEPTH, 1, D)
    for layer in range(DEPTH):
        j = layer // 2
        if layer % 2 == 0:
            main, cmp0, cmp1 = _norm_proj(x2, g_mix, layer, w_in_even, j, (EVEN_MAIN, LANES, LANES))
            pos, cw1, cw2 = _compress_weights(cmp_pos_k[j], cmp_w1_k[j], cmp_w2_k[j],
                                              cmp_pos_v[j], cmp_w1_v[j], cmp_w2_v[j])
            kvcmp = _compress(cmp0, cmp1, pos, cw1, cw2, B, S)
            o_r, o_n = _even_mixers(main, kvcmp, cosf, sinf, B, S)
            mixes, wo_stack = (o_r, o_n), w_out_even
        else:
            (proj,) = _norm_proj(x2, g_mix, layer, w_in_odd, j, (4 * D,))
            o_h = _hgrn(proj, hgrn_lb_logits, hgrn_norm_g[j], j, B, S)
            mixes, wo_stack = (o_h,), w_out_odd
        x2 = _mix_ffn(x2, mixes, wo_stack, j, g_ffn, layer, w1s, w3s, w2s,
                      final_g=final_norm_g if layer == DEPTH - 1 else None)
    return x2.reshape(B, S, D)
```
